```python
import functools
import jax, jax.numpy as jnp
from jax import lax
import numpy as np

D_MODEL = 2048
BATCH = 2
SEQ = 4096
DEPTH = 1
DEC_BATCH = 128
DEC_SEQ = 4
PAST_LEN = 8192
PAGE_SIZE = 128

N_META = 16
RWKV_HEAD = 64
RWKV_HEADS = D_MODEL // RWKV_HEAD
RWKV_DIM = RWKV_HEADS * RWKV_HEAD
DECAY_LORA = 96
AAA_LORA = 96
GATE_LORA = 256
RWKV_COLS = 3 * RWKV_DIM + DECAY_LORA + AAA_LORA + GATE_LORA
RWKV_GN_EPS = 64e-5
MLA_HEADS = 16
QK_NOPE = 128
QK_ROPE = 64
QK_DIM = QK_NOPE + QK_ROPE
V_HEAD = 128
KV_LORA = 512
MLA_COLS = MLA_HEADS * QK_DIM + KV_LORA + QK_ROPE
GATE_COLS = 2 * D_MODEL
IN_COLS = RWKV_COLS + MLA_COLS + GATE_COLS
ROPE_THETA = 10000.0
Q_BLOCK = 128
N_EXPERTS = 32
TOP_K = 4
D_FF = D_MODEL
SWIGLU_LIMIT = 7.0
SWIGLU_ALPHA = 1.702
MOE_BLOCK = 128
NORM_EPS = 1e-6
NEG_INF = -1e30

kernel_name = 'rwkv7_mla_gated_moe_step'


def rmsnorm(x, g):
    xf = x.astype(jnp.float32)
    y = xf * lax.rsqrt(jnp.mean(xf * xf, axis=-1, keepdims=True) + NORM_EPS)
    return (y * g.astype(jnp.float32)).astype(x.dtype)


def rope_tables(pos):
    half = QK_ROPE // 2
    inv = ROPE_THETA ** (-jnp.arange(half, dtype=jnp.float32) / half)
    ang = pos.astype(jnp.float32)[:, None] * inv[None, :]
    return jnp.cos(ang), jnp.sin(ang)


def apply_rope(x, cos, sin):
    half = QK_ROPE // 2
    x1 = x[..., :half].astype(jnp.float32)
    x2 = x[..., half:].astype(jnp.float32)
    c = cos[:, None, :]
    s = sin[:, None, :]
    return jnp.concatenate([x1 * c - x2 * s, x2 * c + x1 * s], axis=-1).astype(x.dtype)


def qk_norm(x, g):
    gain = jnp.concatenate([g[:QK_NOPE], g[QK_NOPE:], g[QK_NOPE:]])
    return rmsnorm(x, gain)


def mla_keys(c, kpe, w_kv_up, k_norm_g):
    w_uk = w_kv_up.reshape(KV_LORA, MLA_HEADS, QK_NOPE + V_HEAD)[..., :QK_NOPE]
    k_nope = jnp.einsum('...lc,chd->...lhd', c, w_uk)
    k_pe = jnp.broadcast_to(kpe[..., None, :], k_nope.shape[:-1] + (QK_ROPE,)).astype(k_nope.dtype)
    return qk_norm(jnp.concatenate([k_nope, k_pe], axis=-1), k_norm_g)


def mla_attend(q, k, c, q_pos, k_pos, w_kv_up):
    w_uv = w_kv_up.reshape(KV_LORA, MLA_HEADS, QK_NOPE + V_HEAD)[..., QK_NOPE:]
    s = jnp.einsum('...qhd,...khd->...hqk', q, k).astype(jnp.float32) * (QK_DIM ** -0.5)
    s = jnp.where(k_pos[None, :] <= q_pos[:, None], s, NEG_INF)
    p = jax.nn.softmax(s, axis=-1).astype(c.dtype)
    o_lat = jnp.einsum('...hqk,...kc->...qhc', p, c)
    return jnp.einsum('...qhc,chd->...qhd', o_lat, w_uv)


def attend_prompt(q, c, kpe, pos, lw):
    n, t = q.shape[0], q.shape[1]
    t_pad = -(-t // Q_BLOCK) * Q_BLOCK
    k = mla_keys(c, kpe, lw['w_kv_up'], lw['k_norm_g'])
    pad4 = ((0, 0), (0, t_pad - t), (0, 0), (0, 0))
    k = jnp.pad(k, pad4)
    qp = jnp.pad(q, pad4)
    cp = jnp.pad(c, ((0, 0), (0, t_pad - t), (0, 0)))
    k_pos = jnp.arange(t_pad)
    qb = jnp.moveaxis(qp.reshape(n, t_pad // Q_BLOCK, Q_BLOCK, MLA_HEADS, QK_DIM), 1, 0)
    qpos = k_pos.reshape(t_pad // Q_BLOCK, Q_BLOCK)
    ob = lax.map(lambda a: mla_attend(a[0], k, cp, a[1], k_pos, lw['w_kv_up']), (qb, qpos))
    return jnp.moveaxis(ob, 0, 1).reshape(n, t_pad, MLA_HEADS, V_HEAD)[:, :t]


def attend_paged(q, c, kpe, pos, lw, cache_lat, cache_kpe, page_table, layer):
    past = page_table.shape[1] * PAGE_SIZE
    k_pos = jnp.arange(past + q.shape[1])

    def per_seq(a):
        pt, qs, cn, kn = a
        c_all = jnp.concatenate([cache_lat[layer, pt].reshape(past, KV_LORA).astype(cn.dtype), cn], axis=0)
        kpe_all = jnp.concatenate([cache_kpe[layer, pt].reshape(past, QK_ROPE).astype(kn.dtype), kn], axis=0)
        k = mla_keys(c_all, kpe_all, lw['w_kv_up'], lw['k_norm_g'])
        return mla_attend(qs, k, c_all, pos, k_pos, lw['w_kv_up'])

    return lax.map(per_seq, (page_table, q, c, kpe))


def wkv_scan(S0, r, decay, k, v, kk, b):
    def step(S, inp):
        r_t, w_t, k_t, v_t, kk_t, b_t = inp
        sa = jnp.einsum('nhvk,nhk->nhv', S, -kk_t)
        S = S * w_t[:, :, None, :] + sa[..., None] * b_t[:, :, None, :] + v_t[..., None] * k_t[:, :, None, :]
        return S, jnp.einsum('nhvk,nhk->nhv', S, r_t)

    xs = tuple(jnp.moveaxis(u, 1, 0) for u in (r, decay, k, v, kk, b))
    S, ys = lax.scan(step, S0, xs)
    return jnp.moveaxis(ys, 0, 1), S


def rwkv7_branch(cols, shift0, S0, lw):
    n, t, _ = cols.shape
    prev = jnp.concatenate([shift0[:, None, :].astype(cols.dtype), cols[:, :-1]], axis=1)
    mixed = cols + lw['rw_mu'] * (prev - cols)
    o1, o2, o3 = RWKV_DIM, 2 * RWKV_DIM, 3 * RWKV_DIM
    r, k, v, w_in, a_in, g_in = jnp.split(mixed, [o1, o2, o3, o3 + DECAY_LORA, o3 + DECAY_LORA + AAA_LORA], axis=-1)
    z = (lw['rw_w0'] + jnp.tanh(w_in) @ lw['rw_w2']).astype(jnp.float32)
    decay = jnp.exp(-jnp.exp(-jax.nn.softplus(-z) - 0.5))
    a = jax.nn.sigmoid((lw['rw_a0'] + a_in @ lw['rw_a2']).astype(jnp.float32))
    g = jax.nn.sigmoid(g_in) @ lw['rw_g2']
    hshape = (n, t, RWKV_HEADS, RWKV_HEAD)
    r, k, v, decay, a = (u.astype(jnp.float32).reshape(hshape) for u in (r, k, v, decay, a))
    k_k = lw['rw_k_k'].astype(jnp.float32).reshape(RWKV_HEADS, RWKV_HEAD)
    k_a = lw['rw_k_a'].astype(jnp.float32).reshape(RWKV_HEADS, RWKV_HEAD)
    kk = k * k_k
    kk = kk * lax.rsqrt(jnp.maximum(jnp.sum(kk * kk, axis=-1, keepdims=True), 1e-24))
    k = k * (1.0 + (a - 1.0) * k_a)
    y, S = wkv_scan(S0.astype(jnp.float32), r, decay, k, v, kk, kk * a)
    mu = jnp.mean(y, axis=-1, keepdims=True)
    var = jnp.mean(jnp.square(y - mu), axis=-1, keepdims=True)
    ln_g = lw['rw_ln_g'].astype(jnp.float32).reshape(RWKV_HEADS, RWKV_HEAD)
    ln_b = lw['rw_ln_b'].astype(jnp.float32).reshape(RWKV_HEADS, RWKV_HEAD)
    y = (y - mu) * lax.rsqrt(var + RWKV_GN_EPS) * ln_g + ln_b
    y = y + jnp.sum(r * k * lw['rw_r_k'].astype(jnp.float32), axis=-1, keepdims=True) * v
    y = y.reshape(n, t, RWKV_DIM).astype(cols.dtype) * g
    return y, S, cols[:, -1]


def moe_ffn(x, lw):
    n = x.shape[0]
    logits = (x @ lw['w_router'] + lw['b_router']).astype(jnp.float32)
    top_val, top_idx = lax.top_k(logits, TOP_K)
    gate = jax.nn.softmax(top_val, axis=-1)
    m = n * TOP_K
    e_flat = top_idx.reshape(m)
    tok_flat = jnp.arange(m, dtype=jnp.int32) // TOP_K
    gate_flat = gate.reshape(m)
    order = jnp.argsort(e_flat)
    e_sorted = e_flat[order]
    counts = jnp.zeros((N_EXPERTS,), jnp.int32).at[e_flat].add(1)
    starts = jnp.cumsum(counts) - counts
    padded = ((counts + MOE_BLOCK - 1) // MOE_BLOCK) * MOE_BLOCK
    pad_ends = jnp.cumsum(padded)
    pad_starts = pad_ends - padded
    slot = pad_starts[e_sorted] + (jnp.arange(m, dtype=jnp.int32) - starts[e_sorted])
    n_blocks = -(-m // MOE_BLOCK) + N_EXPERTS
    cap = n_blocks * MOE_BLOCK
    slot_tok = jnp.full((cap,), n, jnp.int32).at[slot].set(tok_flat[order])
    slot_gate = jnp.zeros((cap,), jnp.float32).at[slot].set(gate_flat[order])
    block_start = jnp.arange(n_blocks, dtype=jnp.int32) * MOE_BLOCK
    block_e = jnp.minimum(jnp.searchsorted(pad_ends, block_start, side='right'), N_EXPERTS - 1)
    x_pad = jnp.concatenate([x, jnp.zeros((1, x.shape[1]), x.dtype)], axis=0)
    xb = x_pad[slot_tok].reshape(n_blocks, MOE_BLOCK, x.shape[1])

    def expert_block(inp):
        xe, e = inp
        h = xe @ lw['w_gu'][e] + lw['b_gu'][e]
        gt = jnp.minimum(h[..., :D_FF], SWIGLU_LIMIT)
        up = jnp.clip(h[..., D_FF:], -SWIGLU_LIMIT, SWIGLU_LIMIT)
        act = (up + 1.0) * gt * jax.nn.sigmoid(SWIGLU_ALPHA * gt)
        return act @ lw['w_down'][e] + lw['b_down'][e]

    yb = lax.map(expert_block, (xb, block_e)).reshape(cap, x.shape[1])
    y = jax.ops.segment_sum(yb * slot_gate[:, None].astype(yb.dtype), slot_tok, num_segments=n + 1)
    return y[:n]


def trunk_layer(x, pos, shift0, S0, attend, lw):
    n, t, _ = x.shape
    xn = rmsnorm(x, lw['ln1_g'])
    cols = xn @ lw['w_in']
    c1 = RWKV_COLS
    c2 = c1 + MLA_HEADS * QK_DIM
    c3 = c2 + KV_LORA
    c4 = c3 + QK_ROPE
    rw_cols, q_raw, c_raw, kpe_raw, gate_a, gate_b = jnp.split(cols, [c1, c2, c3, c4, c4 + D_MODEL], axis=-1)
    ya, S, shift_last = rwkv7_branch(rw_cols, shift0, S0, lw)
    cos, sin = rope_tables(pos)
    q = q_raw.reshape(n, t, MLA_HEADS, QK_DIM)
    q = qk_norm(jnp.concatenate([q[..., :QK_NOPE], apply_rope(q[..., QK_NOPE:], cos, sin)], axis=-1), lw['q_norm_g'])
    c = rmsnorm(c_raw, lw['kv_norm_g'])
    kpe = apply_rope(kpe_raw[:, :, None, :], cos, sin)[:, :, 0, :]
    o = attend(q, c, kpe, pos, lw)
    yb = o.reshape(n, t, MLA_HEADS * V_HEAD) @ lw['w_branch_b']
    ya = ya @ lw['w_branch_a']
    merged = jax.nn.sigmoid(gate_a) * ya + jax.nn.sigmoid(gate_b) * yb
    x = x + merged @ lw['w_out']
    h = rmsnorm(x, lw['ln2_g']).reshape(n * t, D_MODEL)
    x = x + moe_ffn(h, lw).reshape(n, t, D_MODEL)
    return x, c, kpe, S, shift_last


def setup_inputs(seed: int = 0) -> dict:
    key = jax.random.key(seed)
    ks = iter(jax.random.split(key, 48))

    def nrm(shape, scale):
        return scale * jax.random.normal(next(ks), shape, jnp.float32)

    n_pages = PAST_LEN // PAGE_SIZE
    n_phys = (DEC_BATCH * n_pages * 5) // 4
    page_table = jax.random.permutation(next(ks), n_phys)[: DEC_BATCH * n_pages].reshape(DEC_BATCH, n_pages).astype(jnp.int32)
    L = DEPTH
    return {
        'x_prompt': nrm((BATCH, SEQ, D_MODEL), 1.0),
        'x_sample': nrm((DEC_BATCH, DEC_SEQ, D_MODEL), 1.0),
        'cache_kv_latent': nrm((L, n_phys, PAGE_SIZE, KV_LORA), 1.0),
        'cache_k_rope': nrm((L, n_phys, PAGE_SIZE, QK_ROPE), 1.0),
        'state_wkv': nrm((L, DEC_BATCH, RWKV_HEADS, RWKV_HEAD, RWKV_HEAD), 0.3),
        'state_shift': nrm((L, DEC_BATCH, RWKV_COLS), 1.0),
        'page_table': page_table,
        'meta_tokens': nrm((N_META, D_MODEL), 1.0),
        'ln1_g': 1.0 + nrm((L, D_MODEL), 0.02),
        'w_in': nrm((L, D_MODEL, IN_COLS), D_MODEL ** -0.5),
        'rw_mu': jax.random.uniform(next(ks), (L, RWKV_COLS), jnp.float32, 0.1, 0.9),
        'rw_w0': nrm((L, RWKV_DIM), 0.5),
        'rw_w2': nrm((L, DECAY_LORA, RWKV_DIM), 0.5 * DECAY_LORA ** -0.5),
        'rw_a0': nrm((L, RWKV_DIM), 0.1),
        'rw_a2': nrm((L, AAA_LORA, RWKV_DIM), AAA_LORA ** -0.5),
        'rw_g2': nrm((L, GATE_LORA, RWKV_DIM), GATE_LORA ** -0.5),
        'rw_k_k': 0.85 + nrm((L, RWKV_DIM), 0.05),
        'rw_k_a': 1.0 + nrm((L, RWKV_DIM), 0.05),
        'rw_r_k': nrm((L, RWKV_HEADS, RWKV_HEAD), 0.1),
        'rw_ln_g': 1.0 + nrm((L, RWKV_DIM), 0.02),
        'rw_ln_b': nrm((L, RWKV_DIM), 0.02),
        'q_norm_g': 1.0 + nrm((L, QK_NOPE + QK_ROPE // 2), 0.02),
        'k_norm_g': 1.0 + nrm((L, QK_NOPE + QK_ROPE // 2), 0.02),
        'kv_norm_g': 1.0 + nrm((L, KV_LORA), 0.02),
        'w_kv_up': nrm((L, KV_LORA, MLA_HEADS * (QK_NOPE + V_HEAD)), KV_LORA ** -0.5),
        'w_branch_a': nrm((L, RWKV_DIM, D_MODEL), RWKV_DIM ** -0.5),
        'w_branch_b': nrm((L, MLA_HEADS * V_HEAD, D_MODEL), (MLA_HEADS * V_HEAD) ** -0.5),
        'w_out': nrm((L, D_MODEL, D_MODEL), D_MODEL ** -0.5),
        'ln2_g': 1.0 + nrm((L, D_MODEL), 0.02),
        'w_router': nrm((L, D_MODEL, N_EXPERTS), D_MODEL ** -0.5),
        'b_router': nrm((L, N_EXPERTS), 0.01),
        'w_gu': nrm((L, N_EXPERTS, D_MODEL, 2 * D_FF), D_MODEL ** -0.5),
        'b_gu': nrm((L, N_EXPERTS, 2 * D_FF), 0.01),
        'w_down': nrm((L, N_EXPERTS, D_FF, D_MODEL), D_FF ** -0.5),
        'b_down': nrm((L, N_EXPERTS, D_MODEL), 0.01),
    }


def reference(x_prompt, x_sample, cache_kv_latent, cache_k_rope, state_wkv, state_shift, page_table, meta_tokens,
              ln1_g, w_in, rw_mu, rw_w0, rw_w2, rw_a0, rw_a2, rw_g2, rw_k_k, rw_k_a, rw_r_k, rw_ln_g, rw_ln_b,
              q_norm_g, k_norm_g, kv_norm_g, w_kv_up, w_branch_a, w_branch_b, w_out, ln2_g,
              w_router, b_router, w_gu, b_gu, w_down, b_down):
    nb = x_prompt.shape[0]
    past_len = page_table.shape[1] * PAGE_SIZE
    meta = jnp.broadcast_to(meta_tokens[None].astype(x_prompt.dtype), (nb, N_META, D_MODEL))
    xp = jnp.concatenate([meta, x_prompt], axis=1)
    xs = x_sample
    pos_p = jnp.arange(xp.shape[1])
    pos_s = past_len + jnp.arange(xs.shape[1])
    lat_p, kpe_p, wkv_p, sh_p = [], [], [], []
    lat_s, kpe_s, wkv_s, sh_s = [], [], [], []
    for l in range(DEPTH):
        lw = {
            'ln1_g': ln1_g[l], 'w_in': w_in[l], 'rw_mu': rw_mu[l], 'rw_w0': rw_w0[l], 'rw_w2': rw_w2[l],
            'rw_a0': rw_a0[l], 'rw_a2': rw_a2[l], 'rw_g2': rw_g2[l], 'rw_k_k': rw_k_k[l], 'rw_k_a': rw_k_a[l],
            'rw_r_k': rw_r_k[l], 'rw_ln_g': rw_ln_g[l], 'rw_ln_b': rw_ln_b[l], 'q_norm_g': q_norm_g[l],
            'k_norm_g': k_norm_g[l], 'kv_norm_g': kv_norm_g[l], 'w_kv_up': w_kv_up[l],
            'w_branch_a': w_branch_a[l], 'w_branch_b': w_branch_b[l], 'w_out': w_out[l], 'ln2_g': ln2_g[l],
            'w_router': w_router[l], 'b_router': b_router[l], 'w_gu': w_gu[l], 'b_gu': b_gu[l],
            'w_down': w_down[l], 'b_down': b_down[l],
        }
        shift0 = jnp.zeros((nb, RWKV_COLS), xp.dtype)
        wkv0 = jnp.zeros((nb, RWKV_HEADS, RWKV_HEAD, RWKV_HEAD), jnp.float32)
        xp, c1, k1, s1, h1 = trunk_layer(xp, pos_p, shift0, wkv0, attend_prompt, lw)
        attend_s = functools.partial(attend_paged, cache_lat=cache_kv_latent, cache_kpe=cache_k_rope,
                                     page_table=page_table, layer=l)
        xs, c2, k2, s2, h2 = trunk_layer(xs, pos_s, state_shift[l], state_wkv[l], attend_s, lw)
        lat_p.append(c1); kpe_p.append(k1); wkv_p.append(s1); sh_p.append(h1)
        lat_s.append(c2); kpe_s.append(k2); wkv_s.append(s2); sh_s.append(h2)
    y_prompt = xp[:, N_META:]
    y_sample = xs
    return (y_prompt, y_sample, jnp.stack(lat_p), jnp.stack(kpe_p), jnp.stack(wkv_p), jnp.stack(sh_p),
            jnp.stack(lat_s), jnp.stack(kpe_s), jnp.stack(wkv_s), jnp.stack(sh_s))
```

```python
import functools

import numpy as np
import jax
import jax.numpy as jnp
from jax import lax
from jax.experimental import pallas as pl
from jax.experimental.pallas import tpu as pltpu

F32 = jnp.float32
BF16 = jnp.bfloat16

RWKV_HEAD = 64
QK_NOPE = 128
QK_ROPE = 64
QK_DIM = QK_NOPE + QK_ROPE
V_HEAD = 128
QK_PAD = 256
ROPE_THETA = 10000.0
RWKV_GN_EPS = 64e-5
NORM_EPS = 1e-6
NEG_INF = -1e30
TOP_K = 4
SWIGLU_LIMIT = 7.0
SWIGLU_ALPHA = 1.702

VMEM_LIMIT = 56 * 1024 * 1024

ATTN_TILE = 256
MOE_BM = 256
MOE_TF = 512
MOE_TN = 512


def _params(sem):
    return pltpu.CompilerParams(dimension_semantics=sem, vmem_limit_bytes=VMEM_LIMIT)


def _split(x):
    hi = x.astype(BF16)
    lo = (x - hi.astype(F32)).astype(BF16)
    return hi, lo


def _dot2(x, m):
    hi, lo = _split(x)
    return (jnp.dot(hi, m, preferred_element_type=F32) + jnp.dot(lo, m, preferred_element_type=F32))


def _sigmoid(x):
    return 1.0 / (1.0 + jnp.exp(-x))


def _seg_matrix(n, seg):
    e = (np.arange(n)[:, None] // seg == np.arange(n // seg)[None, :]).astype(np.float32)
    return jnp.asarray(e, BF16), jnp.asarray(e.T, BF16)


def _in_proj_kernel(x_ref, g_ref, w_ref, o_ref, xn_ref):
    @pl.when(pl.program_id(1) == 0)
    def _():
        x = x_ref[...]
        ms = jnp.mean(x * x, axis=-1, keepdims=True)
        xn_ref[...] = (x * lax.rsqrt(ms + NORM_EPS) * g_ref[...]).astype(BF16)

    o_ref[...] = jnp.dot(xn_ref[...], w_ref[...], preferred_element_type=F32)


def _in_proj(x, g, w, tm=1024, tn=512):
    r, d = x.shape
    nc = w.shape[1]
    return pl.pallas_call(
        _in_proj_kernel,
        grid=(r // tm, nc // tn),
        in_specs=[pl.BlockSpec((tm, d), lambda i, j: (i, 0)),
                  pl.BlockSpec((1, d), lambda i, j: (0, 0)),
                  pl.BlockSpec((d, tn), lambda i, j: (0, j))],
        out_specs=pl.BlockSpec((tm, tn), lambda i, j: (i, j)),
        out_shape=jax.ShapeDtypeStruct((r, nc), F32),
        scratch_shapes=[pltpu.VMEM((tm, d), BF16)],
        compiler_params=_params(("parallel", "arbitrary")),
        name="in_proj",
    )(x, g, w)


def _make_rwkv_prep_kernel(sample, tr, tiles_per_seq, t_real, ds):
    def kern(r_ref, k_ref, v_ref, l_ref, xr_ref, xk_ref, xv_ref, xl_ref,
             mur_ref, muk_ref, muv_ref, mul_ref, w0_ref, w2_ref, a0_ref, a2_ref, g2_ref,
             kk_ref, ka_ref, e_ref, et_ref,
             ro_ref, lwo_ref, ko_ref, vo_ref, kko_ref, bo_ref, go_ref):
        i = pl.program_id(0)

        def mixed(x_ref, extra_ref, mu_ref):
            x = x_ref[...]
            rolled = pltpu.roll(x, 1, axis=0)
            row = lax.broadcasted_iota(jnp.int32, x.shape, 0)
            if sample:
                prev = jnp.where(row % ds == 0, extra_ref[...], rolled)
            else:
                halo = jnp.where((i % tiles_per_seq) == 0, 0.0, extra_ref[7:8, :])
                prev = jnp.where(row == 0, halo, rolled)
            return x + mu_ref[...] * (prev - x)

        r = mixed(r_ref, xr_ref, mur_ref)
        k = mixed(k_ref, xk_ref, muk_ref)
        v = mixed(v_ref, xv_ref, muv_ref)
        lo = mixed(l_ref, xl_ref, mul_ref)
        w_in = lo[:, 0:128]
        a_in = lo[:, 128:256]
        g_in = lo[:, 256:512]
        z = w0_ref[...] + jnp.dot(jnp.tanh(w_in), w2_ref[...], preferred_element_type=F32)
        nz = -z
        softplus = jnp.maximum(nz, 0.0) + jnp.log(1.0 + jnp.exp(-jnp.abs(nz)))
        lw = -jnp.exp(-softplus - 0.5)
        a = _sigmoid(a0_ref[...] + jnp.dot(a_in, a2_ref[...], preferred_element_type=F32))
        g = jnp.dot(_sigmoid(g_in), g2_ref[...], preferred_element_type=F32)
        kk = k * kk_ref[...]
        ssq = _dot2(_dot2(kk * kk, e_ref[...]), et_ref[...])
        kk = kk * lax.rsqrt(jnp.maximum(ssq, 1e-24))
        k2 = k * (1.0 + (a - 1.0) * ka_ref[...])
        b = kk * a
        if not sample:
            row = lax.broadcasted_iota(jnp.int32, (tr, 1), 0) + (i % tiles_per_seq) * tr
            valid = row < t_real
            lw = jnp.where(valid, lw, 0.0)
            k2 = jnp.where(valid, k2, 0.0)
            kk = jnp.where(valid, kk, 0.0)
            b = jnp.where(valid, b, 0.0)
        ro_ref[...] = r
        lwo_ref[...] = lw
        ko_ref[...] = k2
        vo_ref[...] = v
        kko_ref[...] = kk
        bo_ref[...] = b
        go_ref[...] = g

    return kern


def _rwkv_prep(cols, lay, row0, nrows, sample, extras, prm, tiles_per_seq, t_real, ds, tr=128):
    d = lay["d"]
    rb0 = row0 // tr
    cb = lay["cb"]
    e64, et64 = _seg_matrix(d, RWKV_HEAD)

    def colspec(width, blk):
        return pl.BlockSpec((tr, width), lambda i: (rb0 + i, blk))

    in_specs = [colspec(d, cb["r"]), colspec(d, cb["k"]), colspec(d, cb["v"]), colspec(512, cb["lora"])]
    args = [cols, cols, cols, cols]
    if sample:
        for a_, w_ in zip(extras, (d, d, d, 512)):
            in_specs.append(pl.BlockSpec((tr, w_), lambda i: (i, 0)))
            args.append(a_)
    else:
        def halospec(width, blk):
            return pl.BlockSpec((8, width), lambda i: (jnp.maximum((rb0 + i) * (tr // 8) - 1, 0), blk))
        in_specs += [halospec(d, cb["r"]), halospec(d, cb["k"]), halospec(d, cb["v"]), halospec(512, cb["lora"])]
        args += [cols, cols, cols, cols]
    for name in ("mu_r", "mu_k", "mu_v", "mu_l", "w0", "w2", "a0", "a2", "g2", "k_k", "k_a"):
        a_ = prm[name]
        in_specs.append(pl.BlockSpec(a_.shape, lambda i: (0, 0)))
        args.append(a_)
    in_specs += [pl.BlockSpec(e64.shape, lambda i: (0, 0)), pl.BlockSpec(et64.shape, lambda i: (0, 0))]
    args += [e64, et64]
    out = jax.ShapeDtypeStruct((nrows, d), F32)
    return pl.pallas_call(
        _make_rwkv_prep_kernel(sample, tr, tiles_per_seq, t_real, ds),
        grid=(nrows // tr,),
        in_specs=in_specs,
        out_specs=[pl.BlockSpec((tr, d), lambda i: (i, 0))] * 7,
        out_shape=[out] * 7,
        compiler_params=_params(("parallel",)),
        name="rwkv_prep_sample" if sample else "rwkv_prep_prompt",
    )(*args)


def _make_wkv_kernel(ns, nh, tc, t_real):
    def kern(r_ref, lw_ref, k_ref, v_ref, kk_ref, b_ref, s0_ref, y_ref, so_ref, s_ref):
        c = pl.program_id(1)

        @pl.when(c == 0)
        def _():
            for n in range(ns):
                for h in range(nh):
                    s_ref[n * nh + h] = s0_ref[n, h]

        y_ref[...] = jnp.zeros(y_ref.shape, F32)
        hd = RWKV_HEAD
        eye = (lax.broadcasted_iota(jnp.int32, (hd, hd), 0) == lax.broadcasted_iota(jnp.int32, (hd, hd), 1))
        lane2 = lax.broadcasted_iota(jnp.int32, (hd, 2 * hd), 1)
        eye2 = lax.broadcasted_iota(jnp.int32, (hd, 2 * hd), 0) == lane2 % hd
        sub8 = lax.broadcasted_iota(jnp.int32, (8, 2 * hd), 0)
        ngroups = jnp.clip((t_real - c * tc + 7) // 8, 0, tc // 8)

        def group(gi, carry):
            base = pl.multiple_of(gi * 8, 8)
            for n in range(ns):
                r8 = r_ref[n, pl.ds(base, 8), :]
                w8 = jnp.exp(lw_ref[n, pl.ds(base, 8), :])
                k8 = k_ref[n, pl.ds(base, 8), :]
                v8 = v_ref[n, pl.ds(base, 8), :]
                n8 = -kk_ref[n, pl.ds(base, 8), :]
                b8 = b_ref[n, pl.ds(base, 8), :]
                ytiles = [jnp.zeros((8, 2 * hd), F32) for _ in range(nh // 2)]
                for j in range(8):
                    ycols = []
                    for h in range(nh):
                        sl = slice(h * hd, (h + 1) * hd)
                        s = s_ref[n * nh + h]
                        sa = jnp.sum(s * n8[j:j + 1, sl], axis=1, keepdims=True)
                        vcol = jnp.sum(jnp.where(eye, v8[j:j + 1, sl], 0.0), axis=1, keepdims=True)
                        s = s * w8[j:j + 1, sl] + sa * b8[j:j + 1, sl] + vcol * k8[j:j + 1, sl]
                        s_ref[n * nh + h] = s
                        ycols.append(jnp.sum(s * r8[j:j + 1, sl], axis=1, keepdims=True))
                    for hp in range(nh // 2):
                        both = jnp.where(lane2 < hd, ycols[2 * hp], ycols[2 * hp + 1])
                        yrow = jnp.sum(jnp.where(eye2, both, 0.0), axis=0, keepdims=True)
                        ytiles[hp] = jnp.where(sub8 == j, yrow, ytiles[hp])
                for hp in range(nh // 2):
                    y_ref[n, pl.ds(base, 8), hp * 2 * hd:(hp + 1) * 2 * hd] = ytiles[hp]
            return carry

        lax.fori_loop(0, ngroups, group, 0)

        @pl.when(c == pl.num_programs(1) - 1)
        def _():
            for n in range(ns):
                for h in range(nh):
                    so_ref[n, h] = s_ref[n * nh + h]

    return kern


def _wkv_scan(seqs, s0, t_real, ns, tc):
    n, tp, d = seqs[0].shape
    nh = d // RWKV_HEAD
    seq_spec = pl.BlockSpec((ns, tc, d), lambda g, c: (g, c, 0))
    st_spec = pl.BlockSpec((ns, nh, RWKV_HEAD, RWKV_HEAD), lambda g, c: (g, 0, 0, 0))
    return pl.pallas_call(
        _make_wkv_kernel(ns, nh, tc, t_real),
        grid=(n // ns, tp // tc),
        in_specs=[seq_spec] * 6 + [st_spec],
        out_specs=[seq_spec, st_spec],
        out_shape=[jax.ShapeDtypeStruct((n, tp, d), F32),
                   jax.ShapeDtypeStruct((n, nh, RWKV_HEAD, RWKV_HEAD), F32)],
        scratch_shapes=[pltpu.VMEM((ns * nh, RWKV_HEAD, RWKV_HEAD), F32)],
        compiler_params=_params(("parallel", "arbitrary")),
        name="wkv_scan",
    )(*seqs, s0)


def _rwkv_post_kernel(y_ref, r_ref, k_ref, v_ref, g_ref, lng_ref, lnb_ref, rk_ref, e_ref, et_ref, o_ref):
    e = e_ref[...]
    et = et_ref[...]
    inv = 1.0 / RWKV_HEAD
    y = y_ref[...]
    mu = _dot2(_dot2(y, e), et) * inv
    dlt = y - mu
    var = _dot2(_dot2(dlt * dlt, e), et) * inv
    yn = dlt * lax.rsqrt(var + RWKV_GN_EPS) * lng_ref[...] + lnb_ref[...]
    v = v_ref[...]
    bonus = _dot2(_dot2(r_ref[...] * k_ref[...] * rk_ref[...], e), et) * v
    o_ref[...] = ((yn + bonus) * g_ref[...]).astype(BF16)


def _rwkv_post(y, r, k2, v, g, prm, tr=128):
    nrows, d = y.shape
    e64, et64 = _seg_matrix(d, RWKV_HEAD)
    row = pl.BlockSpec((tr, d), lambda i: (i, 0))
    vec = pl.BlockSpec((1, d), lambda i: (0, 0))
    return pl.pallas_call(
        _rwkv_post_kernel,
        grid=(nrows // tr,),
        in_specs=[row] * 5 + [vec] * 3 + [pl.BlockSpec(e64.shape, lambda i: (0, 0)),
                                          pl.BlockSpec(et64.shape, lambda i: (0, 0))],
        out_specs=row,
        out_shape=jax.ShapeDtypeStruct((nrows, d), BF16),
        compiler_params=_params(("parallel",)),
        name="rwkv_post",
    )(y, r, k2, v, g, prm["ln_g"], prm["ln_b"], prm["r_k"], e64, et64)


def _make_mla_prep_kernel(prompt, nh):
    inv_dim = 1.0 / QK_DIM

    def kern(*refs):
        (qn_ref, q1_ref, q2_ref, c_ref, kp_ref, cos_ref, sin_ref, c128_ref, s128_ref,
         gqn_ref, gqr_ref, gkv_ref, e128_ref, et128_ref, e32_ref, et32_ref, p1_ref, p2_ref) = refs[:18]
        if prompt:
            (gkn_ref, gkr_ref, wuk_ref, wuv_ref, qo_ref, co_ref, ko_ref, kout_ref, vout_ref) = refs[18:]
        else:
            (qo_ref, co_ref, ko_ref) = refs[18:]
        scale = QK_DIM ** -0.5
        qn = qn_ref[...]
        cos = cos_ref[...]
        sin = sin_ref[...]
        q1 = q1_ref[...]
        q2 = q2_ref[...]
        r1 = q1 * cos - q2 * sin
        r2 = q2 * cos + q1 * sin
        ssq = _dot2(qn * qn, e128_ref[...]) + _dot2(r1 * r1, e32_ref[...]) + _dot2(r2 * r2, e32_ref[...])
        rf = lax.rsqrt(ssq * inv_dim + NORM_EPS) * scale
        qn = qn * _dot2(rf, et128_ref[...]) * gqn_ref[...]
        rf32 = _dot2(rf, et32_ref[...]) * gqr_ref[...]
        r1 = (r1 * rf32).astype(BF16)
        r2 = (r2 * rf32).astype(BF16)
        qr = (jnp.dot(r1, p1_ref[...], preferred_element_type=F32)
              + jnp.dot(r2, p2_ref[...], preferred_element_type=F32)).astype(BF16)
        qnb = qn.astype(BF16)
        for h in range(nh):
            qo_ref[:, h * QK_PAD:h * QK_PAD + QK_NOPE] = qnb[:, h * QK_NOPE:(h + 1) * QK_NOPE]
            qo_ref[:, h * QK_PAD + QK_NOPE:(h + 1) * QK_PAD] = qr[:, h * 128:(h + 1) * 128]
        c = c_ref[...]
        c = c * lax.rsqrt(jnp.mean(c * c, axis=-1, keepdims=True) + NORM_EPS) * gkv_ref[...]
        co_ref[...] = c
        kp = kp_ref[...]
        lane = lax.broadcasted_iota(jnp.int32, kp.shape, 1)
        swap = jnp.where(lane < QK_ROPE // 2, pltpu.roll(kp, 128 - QK_ROPE // 2, axis=1),
                         pltpu.roll(kp, QK_ROPE // 2, axis=1))
        kr = kp * c128_ref[...] + swap * s128_ref[...]
        ko_ref[...] = kr[:, :QK_ROPE]
        if prompt:
            cb = c.astype(BF16)
            kn = jnp.dot(cb, wuk_ref[...], preferred_element_type=F32)
            ssqk = _dot2(kn * kn, e128_ref[...]) + jnp.sum(kr * kr, axis=-1, keepdims=True)
            rk = _dot2(lax.rsqrt(ssqk * inv_dim + NORM_EPS), et128_ref[...])
            knb = (kn * rk * gkn_ref[...]).astype(BF16)
            krg = kr * gkr_ref[...]
            krb = (jnp.concatenate([krg] * nh, axis=1) * rk).astype(BF16)
            for h in range(nh):
                kout_ref[:, h * QK_PAD:h * QK_PAD + QK_NOPE] = knb[:, h * QK_NOPE:(h + 1) * QK_NOPE]
                kout_ref[:, h * QK_PAD + QK_NOPE:(h + 1) * QK_PAD] = krb[:, h * 128:(h + 1) * 128]
            vout_ref[...] = jnp.dot(cb, wuv_ref[...], preferred_element_type=F32).astype(BF16)

    return kern


def _mla_prep(cols, lay, row0, nrows, prompt, tabs, prm, tr=256):
    nh = lay["mla_heads"]
    rb0 = row0 // tr
    cb = lay["cb"]
    dq = nh * QK_NOPE
    dr = nh * (QK_ROPE // 2)
    kvl = prm["kv_g"].shape[1]

    def colspec(width, blk):
        return pl.BlockSpec((tr, width), lambda i: (rb0 + i, blk))

    def tabspec(width):
        return pl.BlockSpec((tr, width), lambda i: (rb0 + i, 0))

    def full(a_):
        return pl.BlockSpec(a_.shape, lambda i: (0,) * a_.ndim)

    e128, et128 = _seg_matrix(dq, QK_NOPE)
    e32, et32 = _seg_matrix(dr, QK_ROPE // 2)
    idx = np.arange(dr)
    p1 = np.zeros((dr, nh * 128), np.float32)
    p2 = np.zeros((dr, nh * 128), np.float32)
    p1[idx, (idx // 32) * 128 + idx % 32] = 1.0
    p2[idx, (idx // 32) * 128 + 32 + idx % 32] = 1.0
    p1 = jnp.asarray(p1, BF16)
    p2 = jnp.asarray(p2, BF16)

    args = [cols, cols, cols, cols, cols, tabs["cos"], tabs["sin"], tabs["c128"], tabs["s128"],
            prm["gq_n"], prm["gq_r"], prm["kv_g"], e128, et128, e32, et32, p1, p2]
    in_specs = [colspec(dq, cb["q_nope"]), colspec(dr, cb["q_r1"]), colspec(dr, cb["q_r2"]),
                colspec(kvl, cb["c"]), colspec(128, cb["kpe"]),
                tabspec(dr), tabspec(dr), tabspec(128), tabspec(128)]
    in_specs += [full(a_) for a_ in args[9:]]
    row = lambda w_: pl.BlockSpec((tr, w_), lambda i: (i, 0))
    out_specs = [row(nh * QK_PAD), row(kvl), row(QK_ROPE)]
    out_shape = [jax.ShapeDtypeStruct((nrows, nh * QK_PAD), BF16),
                 jax.ShapeDtypeStruct((nrows, kvl), F32),
                 jax.ShapeDtypeStruct((nrows, QK_ROPE), F32)]
    if prompt:
        extra = [prm["gk_n"], prm["gk_r128"], prm["w_uk"], prm["w_uv"]]
        args += extra
        in_specs += [full(a_) for a_ in extra]
        out_specs += [row(nh * QK_PAD), row(nh * V_HEAD)]
        out_shape += [jax.ShapeDtypeStruct((nrows, nh * QK_PAD), BF16),
                      jax.ShapeDtypeStruct((nrows, nh * V_HEAD), BF16)]
    return pl.pallas_call(
        _make_mla_prep_kernel(prompt, nh),
        grid=(nrows // tr,),
        in_specs=in_specs,
        out_specs=out_specs,
        out_shape=out_shape,
        compiler_params=_params(("parallel",)),
        name="mla_prep_prompt" if prompt else "mla_prep_sample",
    )(*args)


def _make_prompt_attn_kernel(tq, tk):
    def kern(q_ref, k_ref, v_ref, o_ref):
        qi = pl.program_id(2)
        q = q_ref[...]
        qpos = qi * tq + lax.broadcasted_iota(jnp.int32, (tq, tk), 0)
        kiota = lax.broadcasted_iota(jnp.int32, (tq, tk), 1)

        def body(j, carry):
            m, l, acc = carry
            start = pl.multiple_of(j * tk, tk)
            kj = k_ref[pl.ds(start, tk), :]
            vj = v_ref[pl.ds(start, tk), :]
            s = lax.dot_general(q, kj, (((1,), (1,)), ((), ())), preferred_element_type=F32)
            s = jnp.where(kiota + j * tk <= qpos, s, NEG_INF)
            m_new = jnp.maximum(m, jnp.max(s, axis=1, keepdims=True))
            alpha = jnp.exp(m - m_new)
            p = jnp.exp(s - m_new)
            l = alpha * l + jnp.sum(p, axis=1, keepdims=True)
            acc = alpha * acc + jnp.dot(p.astype(BF16), vj, preferred_element_type=F32)
            return m_new, l, acc

        nkv = (qi * tq + tq + tk - 1) // tk
        m0 = jnp.full((tq, 1), NEG_INF, F32)
        l0 = jnp.zeros((tq, 1), F32)
        a0 = jnp.zeros((tq, V_HEAD), F32)
        m, l, acc = lax.fori_loop(0, nkv, body, (m0, l0, a0))
        o_ref[...] = (acc / l).astype(BF16)

    return kern


def _prompt_attn(q, k, v, nb, tp, nh, tq=ATTN_TILE, tk=ATTN_TILE):
    nq = tp // tq
    return pl.pallas_call(
        _make_prompt_attn_kernel(tq, tk),
        grid=(nb, nh, nq),
        in_specs=[pl.BlockSpec((tq, QK_PAD), lambda b, h, i: (b * nq + i, h)),
                  pl.BlockSpec((tp, QK_PAD), lambda b, h, i: (b, h)),
                  pl.BlockSpec((tp, V_HEAD), lambda b, h, i: (b, h))],
        out_specs=pl.BlockSpec((tq, V_HEAD), lambda b, h, i: (b * nq + i, h)),
        out_shape=jax.ShapeDtypeStruct((nb * tp, nh * V_HEAD), BF16),
        compiler_params=_params(("parallel", "parallel", "arbitrary")),
        name="prompt_attn",
    )(q, k, v)


def _sample_q_kernel(q_ref, gkn_ref, gkr_ref, wuk_ref, qa_ref, qr_ref):
    q = q_ref[...].astype(F32)
    qn = (q[:, :QK_NOPE] * gkn_ref[...]).astype(BF16)
    qa_ref[...] = lax.dot_general(qn, wuk_ref[...], (((1,), (1,)), ((), ())),
                                  preferred_element_type=F32).astype(BF16)
    qr_ref[...] = (q[:, QK_NOPE:] * gkr_ref[...]).astype(BF16)


def _sample_q(q, prm, nh):
    rs = q.shape[0]
    kvl = prm["w_uk"].shape[0]
    return pl.pallas_call(
        _sample_q_kernel,
        grid=(nh,),
        in_specs=[pl.BlockSpec((rs, QK_PAD), lambda h: (0, h)),
                  pl.BlockSpec((1, QK_NOPE), lambda h: (0, 0)),
                  pl.BlockSpec((1, 128), lambda h: (0, 0)),
                  pl.BlockSpec((kvl, QK_NOPE), lambda h: (0, h))],
        out_specs=[pl.BlockSpec((rs, kvl), lambda h: (0, h)),
                   pl.BlockSpec((rs, 128), lambda h: (0, h))],
        out_shape=[jax.ShapeDtypeStruct((rs, nh * kvl), BF16),
                   jax.ShapeDtypeStruct((rs, nh * 128), BF16)],
        compiler_params=_params(("parallel",)),
        name="sample_q",
    )(q, prm["gk_n1"], prm["gk_r128"], prm["w_uk"])


def _make_sample_attn_kernel(nh, ds, kvl, page, n_pages, chunk_pages):
    nq = nh * ds
    nw = nh * QK_NOPE
    ct = chunk_pages * page
    nch = n_pages // chunk_pages
    inv_dim = 1.0 / QK_DIM

    def kern(pt_ref, wt_ref, qa_ref, p2_ref, cnew_ref, rnew_ref, cc_ref, cr_ref, o_ref,
             l_ref, cbuf, rbuf, nbc, nbr, sem):
        n = pl.program_id(0)

        def copies(j, slot):
            out = []
            for p in range(chunk_pages):
                pg = pt_ref[n * n_pages + j * chunk_pages + p]
                out.append(pltpu.make_async_copy(cc_ref.at[pg], cbuf.at[slot, pl.ds(p * page, page)],
                                                 sem.at[slot, 2 * p]))
                out.append(pltpu.make_async_copy(cr_ref.at[pg], rbuf.at[slot, pl.ds(p * page, page)],
                                                 sem.at[slot, 2 * p + 1]))
            return out

        for cp in copies(0, 0):
            cp.start()

        @pl.when(n == 0)
        def _():
            l_ref[0:nw, :] = wt_ref[...]

        l_ref[nw:nw + nq, :] = qa_ref[0]

        def process(c, r, carry, mask):
            m, l, acc = carry
            cb = c.astype(BF16)
            rb = r.astype(BF16)
            kt = lax.dot_general(l_ref[...], cb, (((1,), (1,)), ((), ())), preferred_element_type=F32)
            a2 = lax.dot_general(p2_ref[0], rb, (((1,), (1,)), ((), ())), preferred_element_type=F32)
            kn = kt[0:nw].reshape(nh, QK_NOPE, ct)
            ssq = jnp.sum(kn * kn, axis=1)
            kpt = a2[nq:nq + QK_ROPE]
            ssq = ssq + jnp.sum(kpt * kpt, axis=0, keepdims=True)
            rf = lax.rsqrt(ssq * inv_dim + NORM_EPS)
            s = (kt[nw:nw + nq] + a2[0:nq]) * jnp.concatenate([rf] * ds, axis=0)
            if mask is not None:
                s = jnp.where(mask, s, NEG_INF)
            m_new = jnp.maximum(m, jnp.max(s, axis=1, keepdims=True))
            alpha = jnp.exp(m - m_new)
            p = jnp.exp(s - m_new)
            l = alpha * l + jnp.sum(p, axis=1, keepdims=True)
            acc = alpha * acc + jnp.dot(p.astype(BF16), cb, preferred_element_type=F32)
            return m_new, l, acc

        def body(j, carry):
            slot = j % 2

            @pl.when(j + 1 < nch)
            def _():
                for cp in copies(j + 1, 1 - slot):
                    cp.start()

            for cp in copies(j, slot):
                cp.wait()
            return process(cbuf[slot], rbuf[slot], carry, None)

        init = (jnp.full((nq, 1), NEG_INF, F32), jnp.zeros((nq, 1), F32), jnp.zeros((nq, kvl), F32))
        carry = lax.fori_loop(0, nch, body, init)
        nbc[...] = jnp.zeros(nbc.shape, F32)
        nbr[...] = jnp.zeros(nbr.shape, F32)
        nbc[0:8, :] = cnew_ref[0]
        nbr[0:8, :] = rnew_ref[0]
        qidx = lax.broadcasted_iota(jnp.int32, (nq, ct), 0) // nh
        kidx = lax.broadcasted_iota(jnp.int32, (nq, ct), 1)
        m, l, acc = process(nbc[...], nbr[...], carry, kidx <= qidx)
        o_ref[0] = acc / l

    return kern


def _sample_attn(page_table, wt, qa, p2, cnew, rnew, cache_c, cache_r, nh, ds, chunk_pages=2):
    db, n_pages = page_table.shape
    _, page, kvl = cache_c.shape
    nq = nh * ds
    nw = nh * QK_NOPE
    ct = chunk_pages * page
    grid_spec = pltpu.PrefetchScalarGridSpec(
        num_scalar_prefetch=1,
        grid=(db,),
        in_specs=[pl.BlockSpec((nw, kvl), lambda n, pt: (0, 0)),
                  pl.BlockSpec((1, nq, kvl), lambda n, pt: (n, 0, 0)),
                  pl.BlockSpec((1, nq + QK_ROPE, QK_ROPE), lambda n, pt: (n, 0, 0)),
                  pl.BlockSpec((1, 8, kvl), lambda n, pt: (n, 0, 0)),
                  pl.BlockSpec((1, 8, QK_ROPE), lambda n, pt: (n, 0, 0)),
                  pl.BlockSpec(memory_space=pl.ANY),
                  pl.BlockSpec(memory_space=pl.ANY)],
        out_specs=pl.BlockSpec((1, nq, kvl), lambda n, pt: (n, 0, 0)),
        scratch_shapes=[pltpu.VMEM((nw + nq, kvl), BF16),
                        pltpu.VMEM((2, ct, kvl), F32),
                        pltpu.VMEM((2, ct, QK_ROPE), F32),
                        pltpu.VMEM((ct, kvl), F32),
                        pltpu.VMEM((ct, QK_ROPE), F32),
                        pltpu.SemaphoreType.DMA((2, 2 * chunk_pages))],
    )
    return pl.pallas_call(
        _make_sample_attn_kernel(nh, ds, kvl, page, n_pages, chunk_pages),
        grid_spec=grid_spec,
        out_shape=jax.ShapeDtypeStruct((db, nq, kvl), F32),
        compiler_params=_params(("arbitrary",)),
        name="sample_attn",
    )(page_table.reshape(-1), wt, qa, p2, cnew, rnew, cache_c, cache_r)


def _sample_o_kernel(ol_ref, wuv_ref, o_ref):
    o_ref[...] = jnp.dot(ol_ref[...].astype(BF16), wuv_ref[...], preferred_element_type=F32).astype(BF16)


def _sample_o(olat, w_uv, nh):
    rs = olat.shape[0]
    kvl = w_uv.shape[0]
    return pl.pallas_call(
        _sample_o_kernel,
        grid=(nh,),
        in_specs=[pl.BlockSpec((rs, kvl), lambda h: (0, h)),
                  pl.BlockSpec((kvl, V_HEAD), lambda h: (0, h))],
        out_specs=pl.BlockSpec((rs, V_HEAD), lambda h: (0, h)),
        out_shape=jax.ShapeDtypeStruct((rs, nh * V_HEAD), BF16),
        compiler_params=_params(("parallel",)),
        name="sample_o",
    )(olat, w_uv)


def _merge_kernel(ya_ref, yb_ref, ga_ref, gb_ref, wa_ref, wb_ref, o_ref):
    pa = jnp.dot(ya_ref[...], wa_ref[...], preferred_element_type=F32)
    pb = jnp.dot(yb_ref[...], wb_ref[...], preferred_element_type=F32)
    o_ref[...] = (_sigmoid(ga_ref[...]) * pa + _sigmoid(gb_ref[...]) * pb).astype(BF16)


def _merge(ya, yb, cols, lay, wa, wb, tm=512, tn=512):
    r, d = ya.shape
    nj = d // tn
    ga0 = lay["cb"]["gate_a"] * (d // tn)
    gb0 = lay["cb"]["gate_b"] * (d // tn)
    return pl.pallas_call(
        _merge_kernel,
        grid=(nj, r // tm),
        in_specs=[pl.BlockSpec((tm, d), lambda j, i: (i, 0)),
                  pl.BlockSpec((tm, d), lambda j, i: (i, 0)),
                  pl.BlockSpec((tm, tn), lambda j, i: (i, ga0 + j)),
                  pl.BlockSpec((tm, tn), lambda j, i: (i, gb0 + j)),
                  pl.BlockSpec((d, tn), lambda j, i: (0, j)),
                  pl.BlockSpec((d, tn), lambda j, i: (0, j))],
        out_specs=pl.BlockSpec((tm, tn), lambda j, i: (i, j)),
        out_shape=jax.ShapeDtypeStruct((r, d), BF16),
        compiler_params=_params(("parallel", "parallel")),
        name="branch_merge",
    )(ya, yb, cols, cols, wa, wb)


def _make_out_router_kernel(n_experts):
    def kern(m_ref, x_ref, wo_ref, g2_ref, wrh_ref, wrl_ref, br_ref, x1_ref, h_ref, gate_ref, idx_ref):
        x1 = x_ref[...] + jnp.dot(m_ref[...], wo_ref[...], preferred_element_type=F32)
        x1_ref[...] = x1
        h = x1 * lax.rsqrt(jnp.mean(x1 * x1, axis=-1, keepdims=True) + NORM_EPS) * g2_ref[...]
        h_ref[...] = h
        hh, hl = _split(h)
        logits = (jnp.dot(hh, wrh_ref[...], preferred_element_type=F32)
                  + jnp.dot(hh, wrl_ref[...], preferred_element_type=F32)
                  + jnp.dot(hl, wrh_ref[...], preferred_element_type=F32)) + br_ref[...]
        lane = lax.broadcasted_iota(jnp.int32, logits.shape, 1)
        work = jnp.where(lane < n_experts, logits, -jnp.inf)
        vals, idxs = [], []
        for _ in range(TOP_K):
            mx = jnp.max(work, axis=1, keepdims=True)
            ix = jnp.min(jnp.where(work == mx, lane, 128), axis=1, keepdims=True)
            vals.append(mx)
            idxs.append(ix)
            work = jnp.where(lane == ix, -jnp.inf, work)
        ex = [jnp.exp(v_ - vals[0]) for v_ in vals]
        den = ex[0] + ex[1] + ex[2] + ex[3]
        gate = jnp.zeros(logits.shape, F32)
        idx = jnp.zeros(logits.shape, jnp.int32)
        for k_ in range(TOP_K):
            gate = jnp.where(lane == k_, ex[k_] / den, gate)
            idx = jnp.where(lane == k_, idxs[k_], idx)
        gate_ref[...] = gate
        idx_ref[...] = idx

    return kern


def _out_router(merged, x, wo, ln2, wr_hi, wr_lo, br, n_experts, tm=256):
    r, d = x.shape
    row = pl.BlockSpec((tm, d), lambda i: (i, 0))
    small = pl.BlockSpec((tm, 128), lambda i: (i, 0))
    full = lambda a_: pl.BlockSpec(a_.shape, lambda i: (0, 0))
    return pl.pallas_call(
        _make_out_router_kernel(n_experts),
        grid=(r // tm,),
        in_specs=[row, row, full(wo), full(ln2), full(wr_hi), full(wr_lo), full(br)],
        out_specs=[row, row, small, small],
        out_shape=[jax.ShapeDtypeStruct((r, d), F32), jax.ShapeDtypeStruct((r, d), F32),
                   jax.ShapeDtypeStruct((r, 128), F32), jax.ShapeDtypeStruct((r, 128), jnp.int32)],
        compiler_params=_params(("parallel",)),
        name="out_proj_router",
    )(merged, x, wo, ln2, wr_hi, wr_lo, br)


def _make_gather_kernel(bm):
    def kern(nused_ref, tok_ref, h_ref, o_ref, buf, sem):
        b = pl.program_id(0)

        @pl.when(b < nused_ref[0])
        def _():
            def issue(i, carry):
                pltpu.make_async_copy(h_ref.at[pl.ds(tok_ref[0, 0, i], 1)], buf.at[pl.ds(i, 1)], sem).start()
                return carry

            lax.fori_loop(0, bm, issue, 0)
            pltpu.make_async_copy(h_ref.at[pl.ds(0, bm)], buf, sem).wait()
            o_ref[...] = buf[...].astype(BF16)

        @pl.when(b >= nused_ref[0])
        def _():
            o_ref[...] = jnp.zeros(o_ref.shape, BF16)

    return kern


def _moe_gather(h, tok_sorted, nused, bm):
    mp = tok_sorted.shape[0]
    d = h.shape[1]
    nb = mp // bm
    grid_spec = pltpu.PrefetchScalarGridSpec(
        num_scalar_prefetch=1,
        grid=(nb,),
        in_specs=[pl.BlockSpec((1, 1, bm), lambda b, nu: (b, 0, 0), memory_space=pltpu.SMEM),
                  pl.BlockSpec(memory_space=pl.ANY)],
        out_specs=pl.BlockSpec((bm, d), lambda b, nu: (b, 0)),
        scratch_shapes=[pltpu.VMEM((bm, d), F32), pltpu.SemaphoreType.DMA(())],
    )
    return pl.pallas_call(
        _make_gather_kernel(bm),
        grid_spec=grid_spec,
        out_shape=jax.ShapeDtypeStruct((mp, d), BF16),
        compiler_params=_params(("arbitrary",)),
        name="moe_gather",
    )(nused, tok_sorted.reshape(nb, 1, bm), h)


WK_E, WK_J, WK_B, WK_JO, WK_VALID, WK_FIRST = range(6)


def _moe_up_kernel(wk_ref, x_ref, wg_ref, wu_ref, bg_ref, bu_ref, o_ref, wgb, wub):
    w = pl.program_id(0)

    @pl.when(wk_ref[WK_VALID, w] == 1)
    def _():
        @pl.when(wk_ref[WK_FIRST, w] == 1)
        def _():
            wgb[...] = wg_ref[0].astype(BF16)
            wub[...] = wu_ref[0].astype(BF16)

        x = x_ref[...]
        hg = jnp.dot(x, wgb[...], preferred_element_type=F32) + bg_ref[0]
        hu = jnp.dot(x, wub[...], preferred_element_type=F32) + bu_ref[0]
        gt = jnp.minimum(hg, SWIGLU_LIMIT)
        up = jnp.clip(hu, -SWIGLU_LIMIT, SWIGLU_LIMIT)
        o_ref[...] = ((up + 1.0) * gt * _sigmoid(SWIGLU_ALPHA * gt)).astype(BF16)

    @pl.when(wk_ref[WK_VALID, w] == 0)
    def _():
        o_ref[...] = jnp.zeros(o_ref.shape, BF16)


def _moe_up(xs, w_gu, b_gu, work, bm, tf):
    mp, d = xs.shape
    ne, _, f2 = w_gu.shape
    dff = f2 // 2
    nj = dff // tf
    nwork = work.shape[1]
    grid_spec = pltpu.PrefetchScalarGridSpec(
        num_scalar_prefetch=1,
        grid=(nwork,),
        in_specs=[pl.BlockSpec((bm, d), lambda w, wk: (wk[WK_B, w], 0)),
                  pl.BlockSpec((1, d, tf), lambda w, wk: (wk[WK_E, w], 0, wk[WK_J, w])),
                  pl.BlockSpec((1, d, tf), lambda w, wk: (wk[WK_E, w], 0, nj + wk[WK_J, w])),
                  pl.BlockSpec((1, 1, tf), lambda w, wk: (wk[WK_E, w], 0, wk[WK_J, w])),
                  pl.BlockSpec((1, 1, tf), lambda w, wk: (wk[WK_E, w], 0, nj + wk[WK_J, w]))],
        out_specs=pl.BlockSpec((bm, tf), lambda w, wk: (wk[WK_B, w], wk[WK_JO, w])),
        scratch_shapes=[pltpu.VMEM((d, tf), BF16), pltpu.VMEM((d, tf), BF16)],
    )
    return pl.pallas_call(
        _moe_up_kernel,
        grid_spec=grid_spec,
        out_shape=jax.ShapeDtypeStruct((mp, dff), BF16),
        compiler_params=_params(("arbitrary",)),
        name="moe_up",
    )(work, xs, w_gu, w_gu, b_gu, b_gu)


def _moe_down_kernel(wk_ref, a_ref, wd_ref, bd_ref, gt_ref, o_ref, wdb):
    w = pl.program_id(0)

    @pl.when(wk_ref[WK_VALID, w] == 1)
    def _():
        @pl.when(wk_ref[WK_FIRST, w] == 1)
        def _():
            wdb[...] = wd_ref[0].astype(BF16)

        y = jnp.dot(a_ref[...], wdb[...], preferred_element_type=F32) + bd_ref[0]
        o_ref[...] = y * gt_ref[...]

    @pl.when(wk_ref[WK_VALID, w] == 0)
    def _():
        o_ref[...] = jnp.zeros(o_ref.shape, F32)


def _moe_down(act, w_down, b_down, gate_sorted, work, bm, tn):
    mp, dff = act.shape
    ne, _, d = w_down.shape
    nwork = work.shape[1]
    grid_spec = pltpu.PrefetchScalarGridSpec(
        num_scalar_prefetch=1,
        grid=(nwork,),
        in_specs=[pl.BlockSpec((bm, dff), lambda w, wk: (wk[WK_B, w], 0)),
                  pl.BlockSpec((1, dff, tn), lambda w, wk: (wk[WK_E, w], 0, wk[WK_J, w])),
                  pl.BlockSpec((1, 1, tn), lambda w, wk: (wk[WK_E, w], 0, wk[WK_J, w])),
                  pl.BlockSpec((bm, 1), lambda w, wk: (wk[WK_B, w], 0))],
        out_specs=pl.BlockSpec((bm, tn), lambda w, wk: (wk[WK_B, w], wk[WK_JO, w])),
        scratch_shapes=[pltpu.VMEM((dff, tn), BF16)],
    )
    return pl.pallas_call(
        _moe_down_kernel,
        grid_spec=grid_spec,
        out_shape=jax.ShapeDtypeStruct((mp, d), F32),
        compiler_params=_params(("arbitrary",)),
        name="moe_down",
    )(work, act, w_down, b_down, gate_sorted)


def _make_combine_kernel(tm):
    def kern(pos_ref, x_ref, ys_ref, o_ref, buf, sem):
        def issue(i, carry):
            for k_ in range(TOP_K):
                pltpu.make_async_copy(ys_ref.at[pl.ds(pos_ref[0, 0, i * TOP_K + k_], 1)],
                                      buf.at[k_, pl.ds(i, 1)], sem).start()
            return carry

        lax.fori_loop(0, tm, issue, 0)
        for k_ in range(TOP_K):
            pltpu.make_async_copy(ys_ref.at[pl.ds(0, tm)], buf.at[k_], sem).wait()
        o_ref[...] = x_ref[...] + ((buf[0] + buf[1]) + (buf[2] + buf[3]))

    return kern


def _moe_combine(x1, ys, pos, tm=128):
    r, d = x1.shape
    return pl.pallas_call(
        _make_combine_kernel(tm),
        grid=(r // tm,),
        in_specs=[pl.BlockSpec((1, 1, tm * TOP_K), lambda i: (i, 0, 0), memory_space=pltpu.SMEM),
                  pl.BlockSpec((tm, d), lambda i: (i, 0)),
                  pl.BlockSpec(memory_space=pl.ANY)],
        out_specs=pl.BlockSpec((tm, d), lambda i: (i, 0)),
        out_shape=jax.ShapeDtypeStruct((r, d), F32),
        scratch_shapes=[pltpu.VMEM((TOP_K, tm, d), F32), pltpu.SemaphoreType.DMA(())],
        compiler_params=_params(("arbitrary",)),
        name="moe_combine",
    )(pos.reshape(r // tm, 1, tm * TOP_K), x1, ys)


def _moe_routing(idx, gate, real_rows, n_rows, n_experts, bm, nj):
    n_real = real_rows.shape[0]
    m = n_real * TOP_K
    nb_total = -(-m // bm) + n_experts
    mp = nb_total * bm
    e_flat = idx[real_rows].reshape(m)
    g_flat = gate[real_rows].reshape(m)
    tok_flat = jnp.repeat(jnp.asarray(real_rows, jnp.int32), TOP_K)
    order = jnp.argsort(e_flat)
    e_sorted = e_flat[order]
    counts = jnp.zeros((n_experts,), jnp.int32).at[e_flat].add(1)
    starts = jnp.cumsum(counts) - counts
    nblk = (counts + bm - 1) // bm
    blk_ends = jnp.cumsum(nblk)
    blk_starts = blk_ends - nblk
    slot = blk_starts[e_sorted] * bm + (jnp.arange(m, dtype=jnp.int32) - starts[e_sorted])
    tok_sorted = jnp.zeros((mp,), jnp.int32).at[slot].set(tok_flat[order])
    gate_sorted = jnp.zeros((mp,), F32).at[slot].set(g_flat[order])
    pos_flat = jnp.zeros((m,), jnp.int32).at[order].set(slot)
    pos = jnp.zeros((n_rows, TOP_K), jnp.int32).at[real_rows].set(pos_flat.reshape(n_real, TOP_K))
    nused = blk_ends[-1]
    w = jnp.arange(nb_total * nj, dtype=jnp.int32)
    valid = w < nused * nj
    wc = jnp.minimum(w, jnp.maximum(nused * nj - 1, 0))
    blk_e = jnp.minimum(jnp.searchsorted(blk_ends, jnp.arange(nb_total, dtype=jnp.int32), side="right"),
                        n_experts - 1).astype(jnp.int32)
    we = blk_e[wc // nj]
    local = wc - nj * blk_starts[we]
    nbe = jnp.maximum(nblk[we], 1)
    wj = local // nbe
    wf = ((local % nbe) == 0) & valid
    wb = jnp.where(valid, blk_starts[we] + local % nbe, w // nj)
    wjo = jnp.where(valid, wj, w % nj)
    work = jnp.stack([we, wj, wb, wjo, valid.astype(jnp.int32), wf.astype(jnp.int32)]).astype(jnp.int32)
    return tok_sorted, gate_sorted, pos, nused.reshape(1).astype(jnp.int32), work


def _layout(d, nh_mla, kvl, n_w, n_a, n_g):
    assert d == 2048 and nh_mla * QK_NOPE == d and kvl == 512 and n_g == 256 and n_w <= 128 and n_a <= 128
    cb = {"q_nope": 0, "q_r1": 4, "q_r2": 5, "c": 6, "kpe": 28, "gate_a": 2, "gate_b": 3,
          "r": 4, "k": 5, "v": 6, "lora": 28}
    return {"d": d, "mla_heads": nh_mla, "cb": cb, "nc": 14848}


def _relayout_w_in(w, d, nh, kvl, n_w, n_a, n_g):
    o_w = 3 * d
    o_a = o_w + n_w
    o_g = o_a + n_a
    c1 = o_g + n_g
    c2 = c1 + nh * QK_DIM
    c3 = c2 + kvl
    c4 = c3 + QK_ROPE
    q = w[:, c1:c2].reshape(d, nh, QK_DIM)
    z = lambda n_: jnp.zeros((d, n_), w.dtype)
    half = QK_ROPE // 2
    parts = [q[:, :, :QK_NOPE].reshape(d, nh * QK_NOPE),
             q[:, :, QK_NOPE:QK_NOPE + half].reshape(d, nh * half),
             q[:, :, QK_NOPE + half:].reshape(d, nh * half),
             w[:, c2:c3], w[:, c3:c4], z(128 - QK_ROPE), z(384),
             w[:, c4:c4 + d], w[:, c4 + d:c4 + 2 * d],
             w[:, 0:3 * d],
             w[:, o_w:o_a], z(128 - n_w), w[:, o_a:o_g], z(128 - n_a), w[:, o_g:c1]]
    return jnp.concatenate(parts, axis=1).astype(BF16)


def _pad_lora_vec(vec, d, n_w, n_a, n_g):
    o_w = 3 * d
    o_a = o_w + n_w
    o_g = o_a + n_a
    z = lambda n_: jnp.zeros(vec.shape[:-1] + (n_,), vec.dtype)
    lora = jnp.concatenate([vec[..., o_w:o_a], z(128 - n_w), vec[..., o_a:o_g], z(128 - n_a),
                            vec[..., o_g:o_g + n_g]], axis=-1)
    return vec[..., 0:d], vec[..., d:2 * d], vec[..., 2 * d:3 * d], lora


def _unpad_shift(cols_row, lay, d, n_w, n_a, n_g):
    rkv = cols_row[..., 4 * d:7 * d]
    lo = cols_row[..., 7 * d:7 * d + 512]
    return jnp.concatenate([rkv, lo[..., 0:n_w], lo[..., 128:128 + n_a], lo[..., 256:256 + n_g]], axis=-1)


def _rope_tables(pos, nh):
    half = QK_ROPE // 2
    inv = ROPE_THETA ** (-jnp.arange(half, dtype=F32) / half)
    ang = pos.astype(F32)[:, None] * inv[None, :]
    cos, sin = jnp.cos(ang), jnp.sin(ang)
    z = jnp.zeros((pos.shape[0], 128 - QK_ROPE), F32)
    return {"cos": jnp.tile(cos, (1, nh)), "sin": jnp.tile(sin, (1, nh)),
            "c128": jnp.concatenate([cos, cos, z], axis=1),
            "s128": jnp.concatenate([-sin, sin, z], axis=1)}


def kernel(x_prompt, x_sample, cache_kv_latent, cache_k_rope, state_wkv, state_shift, page_table, meta_tokens,
           ln1_g, w_in, rw_mu, rw_w0, rw_w2, rw_a0, rw_a2, rw_g2, rw_k_k, rw_k_a, rw_r_k, rw_ln_g, rw_ln_b,
           q_norm_g, k_norm_g, kv_norm_g, w_kv_up, w_branch_a, w_branch_b, w_out, ln2_g,
           w_router, b_router, w_gu, b_gu, w_down, b_down):
    depth = w_in.shape[0]
    nb, seq, d = x_prompt.shape
    db, ds, _ = x_sample.shape
    n_meta = meta_tokens.shape[0]
    t = seq + n_meta
    tp = -(-t // ATTN_TILE) * ATTN_TILE
    rp = nb * tp
    rs = db * ds
    r_all = rp + rs
    n_pages = page_table.shape[1]
    page = cache_kv_latent.shape[2]
    kvl = cache_kv_latent.shape[3]
    past = n_pages * page
    nh_rw = d // RWKV_HEAD
    n_w, n_a, n_g = rw_w2.shape[1], rw_a2.shape[1], rw_g2.shape[1]
    nh = w_kv_up.shape[2] // (QK_NOPE + V_HEAD)
    n_experts = w_router.shape[2]
    lay = _layout(d, nh, kvl, n_w, n_a, n_g)
    assert r_all % 1024 == 0 and rs % 256 == 0 and ds <= 8

    meta = jnp.broadcast_to(meta_tokens[None].astype(x_prompt.dtype), (nb, n_meta, d))
    xp = jnp.concatenate([meta, x_prompt, jnp.zeros((nb, tp - t, d), x_prompt.dtype)], axis=1)
    x = jnp.concatenate([xp.reshape(rp, d), x_sample.reshape(rs, d)], axis=0)

    pos_all = jnp.concatenate([jnp.tile(jnp.arange(tp), nb), jnp.tile(past + jnp.arange(ds), db)])
    tabs = _rope_tables(pos_all, nh)
    real_rows = np.concatenate([np.arange(t) + b_ * tp for b_ in range(nb)] + [rp + np.arange(rs)]).astype(np.int32)

    outs = {k_: [] for k_ in ("lat_p", "kpe_p", "wkv_p", "sh_p", "lat_s", "kpe_s", "wkv_s", "sh_s")}
    for l in range(depth):
        w_in_l = _relayout_w_in(w_in[l], d, nh, kvl, n_w, n_a, n_g)
        cols = _in_proj(x, ln1_g[l][None], w_in_l)

        mu_r, mu_k, mu_v, mu_l = _pad_lora_vec(rw_mu[l][None], d, n_w, n_a, n_g)
        zrow = lambda a_, n_: jnp.concatenate([a_, jnp.zeros((128 - n_, d), a_.dtype)], axis=0)
        rprm = {"mu_r": mu_r, "mu_k": mu_k, "mu_v": mu_v, "mu_l": mu_l,
                "w0": rw_w0[l][None], "w2": zrow(rw_w2[l], n_w), "a0": rw_a0[l][None], "a2": zrow(rw_a2[l], n_a),
                "g2": rw_g2[l], "k_k": rw_k_k[l][None], "k_a": rw_k_a[l][None],
                "ln_g": rw_ln_g[l][None], "ln_b": rw_ln_b[l][None], "r_k": rw_r_k[l].reshape(1, d)}
        pp = _rwkv_prep(cols, lay, 0, rp, False, None, rprm, tp // 128, t, ds)
        sh = jnp.zeros((db, ds, state_shift.shape[2]), F32).at[:, 0].set(state_shift[l]).reshape(rs, -1)
        sp = _rwkv_prep(cols, lay, rp, rs, True, _pad_lora_vec(sh, d, n_w, n_a, n_g), rprm, 1, ds, ds)
        r_p, lw_p, k_p, v_p, kk_p, b_p, g_p = pp
        r_s, lw_s, k_s, v_s, kk_s, b_s, g_s = sp
        seq_p = [a_.reshape(nb, tp, d) for a_ in (r_p, lw_p, k_p, v_p, kk_p, b_p)]
        y_p, wkv_p = _wkv_scan(seq_p, jnp.zeros((nb, nh_rw, RWKV_HEAD, RWKV_HEAD), F32), t, nb, 64)
        seq_s = [jnp.pad(a_.reshape(db, ds, d), ((0, 0), (0, 8 - ds), (0, 0)))
                 for a_ in (r_s, lw_s, k_s, v_s, kk_s, b_s)]
        y_s, wkv_s = _wkv_scan(seq_s, state_wkv[l], ds, 2, 8)
        y_s = y_s[:, :ds]
        ya = jnp.concatenate([_rwkv_post(y_p.reshape(rp, d), r_p, k_p, v_p, g_p, rprm),
                              _rwkv_post(y_s.reshape(rs, d), r_s, k_s, v_s, g_s, rprm)], axis=0)

        w_up = w_kv_up[l].reshape(kvl, nh, QK_NOPE + V_HEAD)
        gq, gk = q_norm_g[l], k_norm_g[l]
        half = QK_ROPE // 2
        z64 = jnp.zeros((128 - QK_ROPE,), F32)
        mprm = {"gq_n": jnp.tile(gq[:QK_NOPE], nh)[None], "gq_r": jnp.tile(gq[QK_NOPE:], nh)[None],
                "gk_n": jnp.tile(gk[:QK_NOPE], nh)[None], "gk_n1": gk[:QK_NOPE][None],
                "gk_r128": jnp.concatenate([gk[QK_NOPE:], gk[QK_NOPE:], z64])[None],
                "kv_g": kv_norm_g[l][None],
                "w_uk": w_up[:, :, :QK_NOPE].reshape(kvl, nh * QK_NOPE).astype(BF16),
                "w_uv": w_up[:, :, QK_NOPE:].reshape(kvl, nh * V_HEAD).astype(BF16)}
        q_p, c_p, kpe_p, kx_p, vx_p = _mla_prep(cols, lay, 0, rp, True, tabs, mprm)
        q_s, c_s, kpe_s = _mla_prep(cols, lay, rp, rs, False, tabs, mprm)
        o_p = _prompt_attn(q_p, kx_p, vx_p, nb, tp, nh)

        qa, qr = _sample_q(q_s, mprm, nh)
        qa = qa.reshape(db, ds * nh, kvl)
        qr = qr.reshape(db, ds, nh, 128)[..., :QK_ROPE].reshape(db, ds * nh, QK_ROPE)
        p2 = jnp.concatenate([qr, jnp.broadcast_to(jnp.eye(QK_ROPE, dtype=BF16)[None], (db, QK_ROPE, QK_ROPE))], axis=1)
        pad8 = lambda a_: jnp.concatenate([a_, jnp.zeros((db, 8 - ds, a_.shape[-1]), F32)], axis=1)
        olat = _sample_attn(page_table, mprm["w_uk"].T, qa, p2, pad8(c_s.reshape(db, ds, kvl)),
                            pad8(kpe_s.reshape(db, ds, QK_ROPE)), cache_kv_latent[l], cache_k_rope[l], nh, ds)
        o_s = _sample_o(olat.reshape(rs, nh * kvl), mprm["w_uv"], nh)
        yb = jnp.concatenate([o_p, o_s], axis=0)

        merged = _merge(ya, yb, cols, lay, w_branch_a[l].astype(BF16), w_branch_b[l].astype(BF16))
        wr = jnp.concatenate([w_router[l], jnp.zeros((d, 128 - n_experts), F32)], axis=1)
        wr_hi = wr.astype(BF16)
        wr_lo = (wr - wr_hi.astype(F32)).astype(BF16)
        br = jnp.concatenate([b_router[l], jnp.zeros((128 - n_experts,), F32)])[None]
        x1, hmoe, gate, idx = _out_router(merged, x, w_out[l].astype(BF16), ln2_g[l][None], wr_hi, wr_lo, br, n_experts)

        dff = w_down.shape[2]
        tok_sorted, gate_sorted, pos, nused, work_up = _moe_routing(
            idx[:, :TOP_K], gate[:, :TOP_K], real_rows, r_all, n_experts, MOE_BM, dff // MOE_TF)
        xs = _moe_gather(hmoe, tok_sorted, nused, MOE_BM)
        act = _moe_up(xs, w_gu[l], b_gu[l][:, None, :], work_up, MOE_BM, MOE_TF)
        work_dn = work_up if d // MOE_TN == dff // MOE_TF else _moe_routing(
            idx[:, :TOP_K], gate[:, :TOP_K], real_rows, r_all, n_experts, MOE_BM, d // MOE_TN)[4]
        ys = _moe_down(act, w_down[l], b_down[l][:, None, :], gate_sorted[:, None], work_dn, MOE_BM, MOE_TN)
        x = _moe_combine(x1, ys, pos)

        outs["lat_p"].append(c_p.reshape(nb, tp, kvl)[:, :t])
        outs["kpe_p"].append(kpe_p.reshape(nb, tp, QK_ROPE)[:, :t])
        outs["wkv_p"].append(wkv_p)
        outs["sh_p"].append(_unpad_shift(cols[:rp].reshape(nb, tp, -1)[:, t - 1], lay, d, n_w, n_a, n_g))
        outs["lat_s"].append(c_s.reshape(db, ds, kvl))
        outs["kpe_s"].append(kpe_s.reshape(db, ds, QK_ROPE))
        outs["wkv_s"].append(wkv_s)
        outs["sh_s"].append(_unpad_shift(cols[rp:].reshape(db, ds, -1)[:, ds - 1], lay, d, n_w, n_a, n_g))

    y_prompt = x[:rp].reshape(nb, tp, d)[:, n_meta:t]
    y_sample = x[rp:].reshape(db, ds, d)
    st = lambda k_: jnp.stack(outs[k_])
    return (y_prompt, y_sample, st("lat_p"), st("kpe_p"), st("wkv_p"), st("sh_p"),
            st("lat_s"), st("kpe_s"), st("wkv_s"), st("sh_s"))
```

```python
import functools

import numpy as np
import jax
import jax.numpy as jnp
from jax import lax
from jax.experimental import pallas as pl
from jax.experimental.pallas import tpu as pltpu

F32 = jnp.float32
BF16 = jnp.bfloat16

RWKV_HEAD = 64
QK_NOPE = 128
QK_ROPE = 64
QK_DIM = QK_NOPE + QK_ROPE
V_HEAD = 128
QK_PAD = 256
ROPE_THETA = 10000.0
RWKV_GN_EPS = 64e-5
NORM_EPS = 1e-6
NEG_INF = -1e30
TOP_K = 4
SWIGLU_LIMIT = 7.0
SWIGLU_ALPHA = 1.702

VMEM_LIMIT = 56 * 1024 * 1024

ATTN_TILE = 256
MOE_BM = 256
MOE_TF = 512
MOE_TN = 512


def _params(sem):
    return pltpu.CompilerParams(dimension_semantics=sem, vmem_limit_bytes=VMEM_LIMIT)


def _split(x):
    hi = x.astype(BF16)
    lo = (x - hi.astype(F32)).astype(BF16)
    return hi, lo


def _dot2(x, m):
    hi, lo = _split(x)
    return (jnp.dot(hi, m, preferred_element_type=F32) + jnp.dot(lo, m, preferred_element_type=F32))


def _sigmoid(x):
    return 1.0 / (1.0 + jnp.exp(-x))


def _seg_matrix(n, seg):
    e = (np.arange(n)[:, None] // seg == np.arange(n // seg)[None, :]).astype(np.float32)
    return jnp.asarray(e, BF16), jnp.asarray(e.T, BF16)


def _in_proj_kernel(x_ref, g_ref, w_ref, o_ref, xn_ref):
    @pl.when(pl.program_id(1) == 0)
    def _():
        x = x_ref[...]
        ms = jnp.mean(x * x, axis=-1, keepdims=True)
        xn_ref[...] = (x * lax.rsqrt(ms + NORM_EPS) * g_ref[...]).astype(BF16)

    o_ref[...] = jnp.dot(xn_ref[...], w_ref[...], preferred_element_type=F32)


def _in_proj(x, g, w, tm=1024, tn=512):
    r, d = x.shape
    nc = w.shape[1]
    return pl.pallas_call(
        _in_proj_kernel,
        grid=(r // tm, nc // tn),
        in_specs=[pl.BlockSpec((tm, d), lambda i, j: (i, 0)),
                  pl.BlockSpec((1, d), lambda i, j: (0, 0)),
                  pl.BlockSpec((d, tn), lambda i, j: (0, j))],
        out_specs=pl.BlockSpec((tm, tn), lambda i, j: (i, j)),
        out_shape=jax.ShapeDtypeStruct((r, nc), F32),
        scratch_shapes=[pltpu.VMEM((tm, d), BF16)],
        compiler_params=_params(("parallel", "arbitrary")),
        name="in_proj",
    )(x, g, w)


def _make_rwkv_prep_kernel(sample, tr, tiles_per_seq, t_real, ds):
    def kern(r_ref, k_ref, v_ref, l_ref, xr_ref, xk_ref, xv_ref, xl_ref,
             mur_ref, muk_ref, muv_ref, mul_ref, w0_ref, w2_ref, a0_ref, a2_ref, g2_ref,
             kk_ref, ka_ref, e_ref, et_ref,
             ro_ref, lwo_ref, ko_ref, vo_ref, kko_ref, bo_ref, go_ref):
        i = pl.program_id(0)

        def mixed(x_ref, extra_ref, mu_ref):
            x = x_ref[...]
            rolled = pltpu.roll(x, 1, axis=0)
            row = lax.broadcasted_iota(jnp.int32, x.shape, 0)
            if sample:
                prev = jnp.where(row % ds == 0, extra_ref[...], rolled)
            else:
                halo = jnp.where((i % tiles_per_seq) == 0, 0.0, extra_ref[7:8, :])
                prev = jnp.where(row == 0, halo, rolled)
            return x + mu_ref[...] * (prev - x)

        r = mixed(r_ref, xr_ref, mur_ref)
        k = mixed(k_ref, xk_ref, muk_ref)
        v = mixed(v_ref, xv_ref, muv_ref)
        lo = mixed(l_ref, xl_ref, mul_ref)
        w_in = lo[:, 0:128]
        a_in = lo[:, 128:256]
        g_in = lo[:, 256:512]
        z = w0_ref[...] + jnp.dot(jnp.tanh(w_in), w2_ref[...], preferred_element_type=F32)
        nz = -z
        softplus = jnp.maximum(nz, 0.0) + jnp.log(1.0 + jnp.exp(-jnp.abs(nz)))
        lw = -jnp.exp(-softplus - 0.5)
        a = _sigmoid(a0_ref[...] + jnp.dot(a_in, a2_ref[...], preferred_element_type=F32))
        g = jnp.dot(_sigmoid(g_in), g2_ref[...], preferred_element_type=F32)
        kk = k * kk_ref[...]
        ssq = _dot2(_dot2(kk * kk, e_ref[...]), et_ref[...])
        kk = kk * lax.rsqrt(jnp.maximum(ssq, 1e-24))
        k2 = k * (1.0 + (a - 1.0) * ka_ref[...])
        b = kk * a
        if not sample:
            row = lax.broadcasted_iota(jnp.int32, (tr, 1), 0) + (i % tiles_per_seq) * tr
            valid = row < t_real
            lw = jnp.where(valid, lw, 0.0)
            k2 = jnp.where(valid, k2, 0.0)
            kk = jnp.where(valid, kk, 0.0)
            b = jnp.where(valid, b, 0.0)
        ro_ref[...] = r
        lwo_ref[...] = lw
        ko_ref[...] = k2
        vo_ref[...] = v
        kko_ref[...] = kk
        bo_ref[...] = b
        go_ref[...] = g

    return kern


def _rwkv_prep(cols, lay, row0, nrows, sample, extras, prm, tiles_per_seq, t_real, ds, tr=128):
    d = lay["d"]
    rb0 = row0 // tr
    cb = lay["cb"]
    e64, et64 = _seg_matrix(d, RWKV_HEAD)

    def colspec(width, blk):
        return pl.BlockSpec((tr, width), lambda i: (rb0 + i, blk))

    in_specs = [colspec(d, cb["r"]), colspec(d, cb["k"]), colspec(d, cb["v"]), colspec(512, cb["lora"])]
    args = [cols, cols, cols, cols]
    if sample:
        for a_, w_ in zip(extras, (d, d, d, 512)):
            in_specs.append(pl.BlockSpec((tr, w_), lambda i: (i, 0)))
            args.append(a_)
    else:
        def halospec(width, blk):
            return pl.BlockSpec((8, width), lambda i: (jnp.maximum((rb0 + i) * (tr // 8) - 1, 0), blk))
        in_specs += [halospec(d, cb["r"]), halospec(d, cb["k"]), halospec(d, cb["v"]), halospec(512, cb["lora"])]
        args += [cols, cols, cols, cols]
    for name in ("mu_r", "mu_k", "mu_v", "mu_l", "w0", "w2", "a0", "a2", "g2", "k_k", "k_a"):
        a_ = prm[name]
        in_specs.append(pl.BlockSpec(a_.shape, lambda i: (0, 0)))
        args.append(a_)
    in_specs += [pl.BlockSpec(e64.shape, lambda i: (0, 0)), pl.BlockSpec(et64.shape, lambda i: (0, 0))]
    args += [e64, et64]
    out = jax.ShapeDtypeStruct((nrows, d), F32)
    return pl.pallas_call(
        _make_rwkv_prep_kernel(sample, tr, tiles_per_seq, t_real, ds),
        grid=(nrows // tr,),
        in_specs=in_specs,
        out_specs=[pl.BlockSpec((tr, d), lambda i: (i, 0))] * 7,
        out_shape=[out] * 7,
        compiler_params=_params(("parallel",)),
        name="rwkv_prep_sample" if sample else "rwkv_prep_prompt",
    )(*args)


_NT = (((1,), (1,)), ((), ()))
_TN = (((0,), (0,)), ((), ()))
PAIR = 2 * RWKV_HEAD


def _wkv_masks(c):
    rt = lax.broadcasted_iota(jnp.int32, (2 * c, 2 * c), 0)
    ct = lax.broadcasted_iota(jnp.int32, (2 * c, 2 * c), 1)
    same = (rt // c) == (ct // c)
    strict = same & ((ct % c) < (rt % c))
    incl = same & ((ct % c) <= (rt % c))
    lane = lax.broadcasted_iota(jnp.int32, (c, PAIR), 1)
    return strict, incl, lane < RWKV_HEAD


def _wkv_pair_chunks(items, masks, c):
    strict, incl, m0 = masks
    n = range(len(items))
    c2 = 2 * c
    dot = functools.partial(jnp.dot, preferred_element_type=F32)
    b16 = lambda x: x.astype(BF16)

    def ext(x):
        return jnp.concatenate([jnp.where(m0, x, 0.0), jnp.where(m0, 0.0, x)], axis=0).astype(BF16)

    qr = [jnp.concatenate([ext(it[2]), ext(it[1])], axis=0) for it in items]
    nbk = [jnp.concatenate([ext(-it[3]), ext(it[4])], axis=0) for it in items]
    vx = [ext(it[5]) for it in items]
    a = [lax.dot_general(qr[i], nbk[i], _NT, preferred_element_type=F32) for i in n]
    qg = [lax.dot_general(qr[i], b16(items[i][0]), _NT, preferred_element_type=F32) for i in n]
    lb = [jnp.where(strict, -a[i][0:c2, 0:c2], 0.0) for i in n]
    lk = [b16(jnp.where(strict, a[i][0:c2, c2:2 * c2], 0.0)) for i in n]
    mbk = [b16(jnp.concatenate([jnp.where(incl, a[i][c2:2 * c2, 0:c2], 0.0),
                                jnp.where(incl, a[i][c2:2 * c2, c2:2 * c2], 0.0)], axis=1)) for i in n]
    lp = [b16(x) for x in lb]
    u = [qg[i][0:c2] + dot(lk[i], vx[i]) for i in n]
    sign = -1.0
    span = 1
    while span < c:
        nxt = [dot(lp[i], lp[i]) for i in n] if 2 * span < c else None
        u = [u[i] + sign * dot(lp[i], b16(u[i])) for i in n]
        if nxt is not None:
            lp = [b16(x) for x in nxt]
        sign = 1.0
        span *= 2
    ux = [jnp.concatenate([b16(u[i]), vx[i]], axis=0) for i in n]
    y = [qg[i][c2:2 * c2] + dot(mbk[i], ux[i]) for i in n]
    ds_ = [lax.dot_general(ux[i], nbk[i], _TN, preferred_element_type=F32) for i in n]
    return [(y[i][0:c] + y[i][c:c2], (items[i][0] + ds_[i]) * items[i][6]) for i in n]


def _pair_state(s0, s1):
    z = jnp.zeros((RWKV_HEAD, RWKV_HEAD), F32)
    return jnp.concatenate([jnp.concatenate([s0, z], axis=1), jnp.concatenate([z, s1], axis=1)], axis=0)


def _make_wkv_prompt_kernel(c, nchunk, g, t_real):
    rb = c * nchunk

    def kern(r_ref, lw_ref, k_ref, v_ref, kk_ref, b_ref, y_ref, so_ref, s_ref):
        i = pl.program_id(2)

        @pl.when(i == 0)
        def _():
            s_ref[...] = jnp.zeros(s_ref.shape, F32)

        masks = _wkv_masks(c)
        tril = (lax.broadcasted_iota(jnp.int32, (c, c), 0) >= lax.broadcasted_iota(jnp.int32, (c, c), 1)).astype(BF16)
        nreal = jnp.clip((t_real - i * rb + c - 1) // c, 0, nchunk)

        @pl.when(nreal < nchunk)
        def _():
            y_ref[...] = jnp.zeros(y_ref.shape, F32)

        def chunk(ci, carry):
            rows = pl.ds(pl.multiple_of(ci * c, c), c)
            lw = lw_ref[rows, :]
            hi, lo = _split(lw)
            cs = jnp.dot(tril, hi, preferred_element_type=F32) + jnp.dot(tril, lo, preferred_element_type=F32)
            e_pos = jnp.exp(cs)
            e_neg = jnp.exp(-cs)
            rt = r_ref[rows, :] * e_pos
            qh = kk_ref[rows, :] * jnp.exp(cs - lw)
            bt = b_ref[rows, :] * e_neg
            kt = k_ref[rows, :] * e_neg
            vv = v_ref[rows, :]
            items = []
            for p in range(g):
                sl = slice(p * PAIR, (p + 1) * PAIR)
                items.append((s_ref[p], rt[:, sl], qh[:, sl], bt[:, sl], kt[:, sl], vv[:, sl], e_pos[c - 1:c, sl]))
            outs = _wkv_pair_chunks(items, masks, c)
            for p in range(g):
                s_ref[p] = outs[p][1]
                y_ref[rows, p * PAIR:(p + 1) * PAIR] = outs[p][0]
            return carry

        lax.fori_loop(0, nreal, chunk, 0)

        @pl.when(i == pl.num_programs(2) - 1)
        def _():
            for p in range(g):
                s = s_ref[p]
                so_ref[0, 2 * p] = s[0:RWKV_HEAD, 0:RWKV_HEAD]
                so_ref[0, 2 * p + 1] = s[RWKV_HEAD:PAIR, RWKV_HEAD:PAIR]

    return kern


def _wkv_prompt(seqs, nb, tp, t_real, c=64, nchunk=4, g=8):
    d = seqs[0].shape[1]
    nh = d // RWKV_HEAD
    rb = c * nchunk
    nrb = tp // rb
    spec = pl.BlockSpec((rb, g * PAIR), lambda n, q, i: (n * nrb + i, q))
    return pl.pallas_call(
        _make_wkv_prompt_kernel(c, nchunk, g, t_real),
        grid=(nb, nh // (2 * g), nrb),
        in_specs=[spec] * 6,
        out_specs=[spec, pl.BlockSpec((1, 2 * g, RWKV_HEAD, RWKV_HEAD), lambda n, q, i: (n, q, 0, 0))],
        out_shape=[jax.ShapeDtypeStruct((nb * tp, d), F32),
                   jax.ShapeDtypeStruct((nb, nh, RWKV_HEAD, RWKV_HEAD), F32)],
        scratch_shapes=[pltpu.VMEM((g, PAIR, PAIR), F32)],
        compiler_params=_params(("parallel", "parallel", "arbitrary")),
        name="wkv_prompt",
    )(*seqs)


def _make_wkv_sample_kernel(c, ns, g):
    def kern(r_ref, lw_ref, k_ref, v_ref, kk_ref, b_ref, s0_ref, y_ref, so_ref):
        masks = _wkv_masks(c)
        row = lax.broadcasted_iota(jnp.int32, (c, g * PAIR), 0)
        items = []
        for n in range(ns):
            rows = slice(n * c, (n + 1) * c)
            lw = lw_ref[rows, :]
            cs = jnp.zeros(lw.shape, F32)
            for j in range(c):
                cs = cs + jnp.where(row >= j, lw[j:j + 1, :], 0.0)
            e_pos = jnp.exp(cs)
            e_neg = jnp.exp(-cs)
            rt = r_ref[rows, :] * e_pos
            qh = kk_ref[rows, :] * jnp.exp(cs - lw)
            bt = b_ref[rows, :] * e_neg
            kt = k_ref[rows, :] * e_neg
            vv = v_ref[rows, :]
            for p in range(g):
                sl = slice(p * PAIR, (p + 1) * PAIR)
                s = _pair_state(s0_ref[n, 2 * p], s0_ref[n, 2 * p + 1])
                items.append((s, rt[:, sl], qh[:, sl], bt[:, sl], kt[:, sl], vv[:, sl], e_pos[c - 1:c, sl]))
        outs = _wkv_pair_chunks(items, masks, c)
        for n in range(ns):
            for p in range(g):
                y, s = outs[n * g + p]
                y_ref[n * c:(n + 1) * c, p * PAIR:(p + 1) * PAIR] = y
                so_ref[n, 2 * p] = s[0:RWKV_HEAD, 0:RWKV_HEAD]
                so_ref[n, 2 * p + 1] = s[RWKV_HEAD:PAIR, RWKV_HEAD:PAIR]

    return kern


def _wkv_sample(seqs, s0, c, ns=4, g=4):
    d = seqs[0].shape[1]
    n = s0.shape[0]
    nh = d // RWKV_HEAD
    spec = pl.BlockSpec((ns * c, g * PAIR), lambda i, q: (i, q))
    st = pl.BlockSpec((ns, 2 * g, RWKV_HEAD, RWKV_HEAD), lambda i, q: (i, q, 0, 0))
    return pl.pallas_call(
        _make_wkv_sample_kernel(c, ns, g),
        grid=(n // ns, nh // (2 * g)),
        in_specs=[spec] * 6 + [st],
        out_specs=[spec, st],
        out_shape=[jax.ShapeDtypeStruct((n * c, d), F32),
                   jax.ShapeDtypeStruct((n, nh, RWKV_HEAD, RWKV_HEAD), F32)],
        compiler_params=_params(("parallel", "parallel")),
        name="wkv_sample",
    )(*seqs, s0)


def _rwkv_post_kernel(y_ref, r_ref, k_ref, v_ref, g_ref, lng_ref, lnb_ref, rk_ref, e_ref, et_ref, o_ref):
    e = e_ref[...]
    et = et_ref[...]
    inv = 1.0 / RWKV_HEAD
    y = y_ref[...]
    mu = _dot2(_dot2(y, e), et) * inv
    dlt = y - mu
    var = _dot2(_dot2(dlt * dlt, e), et) * inv
    yn = dlt * lax.rsqrt(var + RWKV_GN_EPS) * lng_ref[...] + lnb_ref[...]
    v = v_ref[...]
    bonus = _dot2(_dot2(r_ref[...] * k_ref[...] * rk_ref[...], e), et) * v
    o_ref[...] = ((yn + bonus) * g_ref[...]).astype(BF16)


def _rwkv_post(y, r, k2, v, g, prm, tr=128):
    nrows, d = y.shape
    e64, et64 = _seg_matrix(d, RWKV_HEAD)
    row = pl.BlockSpec((tr, d), lambda i: (i, 0))
    vec = pl.BlockSpec((1, d), lambda i: (0, 0))
    return pl.pallas_call(
        _rwkv_post_kernel,
        grid=(nrows // tr,),
        in_specs=[row] * 5 + [vec] * 3 + [pl.BlockSpec(e64.shape, lambda i: (0, 0)),
                                          pl.BlockSpec(et64.shape, lambda i: (0, 0))],
        out_specs=row,
        out_shape=jax.ShapeDtypeStruct((nrows, d), BF16),
        compiler_params=_params(("parallel",)),
        name="rwkv_post",
    )(y, r, k2, v, g, prm["ln_g"], prm["ln_b"], prm["r_k"], e64, et64)


def _make_mla_prep_kernel(prompt, nh):
    inv_dim = 1.0 / QK_DIM

    def kern(*refs):
        (qn_ref, q1_ref, q2_ref, c_ref, kp_ref, cos_ref, sin_ref, c128_ref, s128_ref,
         gqn_ref, gqr_ref, gkv_ref, e128_ref, et128_ref, e32_ref, et32_ref, p1_ref, p2_ref) = refs[:18]
        if prompt:
            (gkn_ref, gkr_ref, wuk_ref, wuv_ref, qo_ref, co_ref, ko_ref, kout_ref, vout_ref) = refs[18:]
        else:
            (qo_ref, co_ref, ko_ref) = refs[18:]
        scale = QK_DIM ** -0.5
        qn = qn_ref[...]
        cos = cos_ref[...]
        sin = sin_ref[...]
        q1 = q1_ref[...]
        q2 = q2_ref[...]
        r1 = q1 * cos - q2 * sin
        r2 = q2 * cos + q1 * sin
        ssq = _dot2(qn * qn, e128_ref[...]) + _dot2(r1 * r1, e32_ref[...]) + _dot2(r2 * r2, e32_ref[...])
        rf = lax.rsqrt(ssq * inv_dim + NORM_EPS) * scale
        qn = qn * _dot2(rf, et128_ref[...]) * gqn_ref[...]
        rf32 = _dot2(rf, et32_ref[...]) * gqr_ref[...]
        r1 = (r1 * rf32).astype(BF16)
        r2 = (r2 * rf32).astype(BF16)
        qr = (jnp.dot(r1, p1_ref[...], preferred_element_type=F32)
              + jnp.dot(r2, p2_ref[...], preferred_element_type=F32)).astype(BF16)
        qnb = qn.astype(BF16)
        for h in range(nh):
            qo_ref[:, h * QK_PAD:h * QK_PAD + QK_NOPE] = qnb[:, h * QK_NOPE:(h + 1) * QK_NOPE]
            qo_ref[:, h * QK_PAD + QK_NOPE:(h + 1) * QK_PAD] = qr[:, h * 128:(h + 1) * 128]
        c = c_ref[...]
        c = c * lax.rsqrt(jnp.mean(c * c, axis=-1, keepdims=True) + NORM_EPS) * gkv_ref[...]
        co_ref[...] = c
        kp = kp_ref[...]
        lane = lax.broadcasted_iota(jnp.int32, kp.shape, 1)
        swap = jnp.where(lane < QK_ROPE // 2, pltpu.roll(kp, 128 - QK_ROPE // 2, axis=1),
                         pltpu.roll(kp, QK_ROPE // 2, axis=1))
        kr = kp * c128_ref[...] + swap * s128_ref[...]
        ko_ref[...] = kr[:, :QK_ROPE]
        if prompt:
            cb = c.astype(BF16)
            kn = jnp.dot(cb, wuk_ref[...], preferred_element_type=F32)
            ssqk = _dot2(kn * kn, e128_ref[...]) + jnp.sum(kr * kr, axis=-1, keepdims=True)
            rk = _dot2(lax.rsqrt(ssqk * inv_dim + NORM_EPS), et128_ref[...])
            knb = (kn * rk * gkn_ref[...]).astype(BF16)
            krg = kr * gkr_ref[...]
            krb = (jnp.concatenate([krg] * nh, axis=1) * rk).astype(BF16)
            for h in range(nh):
                kout_ref[:, h * QK_PAD:h * QK_PAD + QK_NOPE] = knb[:, h * QK_NOPE:(h + 1) * QK_NOPE]
                kout_ref[:, h * QK_PAD + QK_NOPE:(h + 1) * QK_PAD] = krb[:, h * 128:(h + 1) * 128]
            vb = jnp.dot(cb, wuv_ref[...], preferred_element_type=F32).astype(BF16)
            ones = jnp.ones((vb.shape[0], V_HEAD), BF16)
            for h in range(nh):
                vout_ref[:, 2 * h * V_HEAD:(2 * h + 1) * V_HEAD] = vb[:, h * V_HEAD:(h + 1) * V_HEAD]
                vout_ref[:, (2 * h + 1) * V_HEAD:(2 * h + 2) * V_HEAD] = ones

    return kern


def _mla_prep(cols, lay, row0, nrows, prompt, tabs, prm, tr=256):
    nh = lay["mla_heads"]
    rb0 = row0 // tr
    cb = lay["cb"]
    dq = nh * QK_NOPE
    dr = nh * (QK_ROPE // 2)
    kvl = prm["kv_g"].shape[1]

    def colspec(width, blk):
        return pl.BlockSpec((tr, width), lambda i: (rb0 + i, blk))

    def tabspec(width):
        return pl.BlockSpec((tr, width), lambda i: (rb0 + i, 0))

    def full(a_):
        return pl.BlockSpec(a_.shape, lambda i: (0,) * a_.ndim)

    e128, et128 = _seg_matrix(dq, QK_NOPE)
    e32, et32 = _seg_matrix(dr, QK_ROPE // 2)
    idx = np.arange(dr)
    p1 = np.zeros((dr, nh * 128), np.float32)
    p2 = np.zeros((dr, nh * 128), np.float32)
    p1[idx, (idx // 32) * 128 + idx % 32] = 1.0
    p2[idx, (idx // 32) * 128 + 32 + idx % 32] = 1.0
    p1 = jnp.asarray(p1, BF16)
    p2 = jnp.asarray(p2, BF16)

    args = [cols, cols, cols, cols, cols, tabs["cos"], tabs["sin"], tabs["c128"], tabs["s128"],
            prm["gq_n"], prm["gq_r"], prm["kv_g"], e128, et128, e32, et32, p1, p2]
    in_specs = [colspec(dq, cb["q_nope"]), colspec(dr, cb["q_r1"]), colspec(dr, cb["q_r2"]),
                colspec(kvl, cb["c"]), colspec(128, cb["kpe"]),
                tabspec(dr), tabspec(dr), tabspec(128), tabspec(128)]
    in_specs += [full(a_) for a_ in args[9:]]
    row = lambda w_: pl.BlockSpec((tr, w_), lambda i: (i, 0))
    out_specs = [row(nh * QK_PAD), row(kvl), row(QK_ROPE)]
    out_shape = [jax.ShapeDtypeStruct((nrows, nh * QK_PAD), BF16),
                 jax.ShapeDtypeStruct((nrows, kvl), F32),
                 jax.ShapeDtypeStruct((nrows, QK_ROPE), F32)]
    if prompt:
        extra = [prm["gk_n"], prm["gk_r128"], prm["w_uk"], prm["w_uv"]]
        args += extra
        in_specs += [full(a_) for a_ in extra]
        out_specs += [row(nh * QK_PAD), row(nh * 2 * V_HEAD)]
        out_shape += [jax.ShapeDtypeStruct((nrows, nh * QK_PAD), BF16),
                      jax.ShapeDtypeStruct((nrows, nh * 2 * V_HEAD), BF16)]
    return pl.pallas_call(
        _make_mla_prep_kernel(prompt, nh),
        grid=(nrows // tr,),
        in_specs=in_specs,
        out_specs=out_specs,
        out_shape=out_shape,
        compiler_params=_params(("parallel",)),
        name="mla_prep_prompt" if prompt else "mla_prep_sample",
    )(*args)


def _make_prompt_attn_kernel(tile, hb):
    def kern(q_ref, k_ref, v_ref, o_ref, m_ref, acc_ref):
        qi = pl.program_id(2)
        m_ref[...] = jnp.full(m_ref.shape, NEG_INF, F32)
        acc_ref[...] = jnp.zeros(acc_ref.shape, F32)
        heads = range(hb)
        qs = [q_ref[:, h * QK_PAD:(h + 1) * QK_PAD] for h in heads]
        causal = (lax.broadcasted_iota(jnp.int32, (tile, tile), 1) <= lax.broadcasted_iota(jnp.int32, (tile, tile), 0))

        def kv_tile(j, diagonal):
            rows = pl.ds(pl.multiple_of(j * tile, tile), tile)
            ss = [lax.dot_general(qs[h], k_ref[rows, h * QK_PAD:(h + 1) * QK_PAD], _NT,
                                  preferred_element_type=F32) for h in heads]
            if diagonal:
                ss = [jnp.where(causal, s, NEG_INF) for s in ss]
            ms = [m_ref[h] for h in heads]
            mn = [jnp.maximum(ms[h], jnp.max(ss[h], axis=1, keepdims=True)) for h in heads]
            ps = [jnp.exp(ss[h] - jnp.concatenate([mn[h]] * (tile // 128), axis=1)) for h in heads]
            pv = [jnp.dot(ps[h].astype(BF16), v_ref[rows, h * 2 * V_HEAD:(h + 1) * 2 * V_HEAD],
                          preferred_element_type=F32) for h in heads]
            for h in heads:
                alpha = jnp.exp(ms[h] - mn[h])
                acc_ref[h] = jnp.concatenate([alpha, alpha], axis=1) * acc_ref[h] + pv[h]
                m_ref[h] = mn[h]

        def body(j, carry):
            kv_tile(j, False)
            return carry

        lax.fori_loop(0, qi, body, 0)
        kv_tile(qi, True)
        for h in heads:
            acc = acc_ref[h]
            o_ref[:, h * V_HEAD:(h + 1) * V_HEAD] = (acc[:, :V_HEAD] / acc[:, V_HEAD:]).astype(BF16)

    return kern


def _prompt_attn(q, k, v, nb, tp, nh, tile=ATTN_TILE, hb=4):
    nq = tp // tile
    return pl.pallas_call(
        _make_prompt_attn_kernel(tile, hb),
        grid=(nb, nh // hb, nq),
        in_specs=[pl.BlockSpec((tile, hb * QK_PAD), lambda b, h, i: (b * nq + i, h)),
                  pl.BlockSpec((tp, hb * QK_PAD), lambda b, h, i: (b, h)),
                  pl.BlockSpec((tp, hb * 2 * V_HEAD), lambda b, h, i: (b, h))],
        out_specs=pl.BlockSpec((tile, hb * V_HEAD), lambda b, h, i: (b * nq + i, h)),
        out_shape=jax.ShapeDtypeStruct((nb * tp, nh * V_HEAD), BF16),
        scratch_shapes=[pltpu.VMEM((hb, tile, 128), F32), pltpu.VMEM((hb, tile, 2 * V_HEAD), F32)],
        compiler_params=_params(("parallel", "parallel", "arbitrary")),
        name="prompt_attn",
    )(q, k, v)


def _sample_q_kernel(q_ref, gkn_ref, gkr_ref, wuk_ref, qa_ref, qr_ref):
    q = q_ref[...].astype(F32)
    qn = (q[:, :QK_NOPE] * gkn_ref[...]).astype(BF16)
    qa_ref[...] = lax.dot_general(qn, wuk_ref[...], (((1,), (1,)), ((), ())),
                                  preferred_element_type=F32).astype(BF16)
    qr_ref[...] = (q[:, QK_NOPE:] * gkr_ref[...]).astype(BF16)


def _sample_q(q, prm, nh):
    rs = q.shape[0]
    kvl = prm["w_uk"].shape[0]
    return pl.pallas_call(
        _sample_q_kernel,
        grid=(nh,),
        in_specs=[pl.BlockSpec((rs, QK_PAD), lambda h: (0, h)),
                  pl.BlockSpec((1, QK_NOPE), lambda h: (0, 0)),
                  pl.BlockSpec((1, 128), lambda h: (0, 0)),
                  pl.BlockSpec((kvl, QK_NOPE), lambda h: (0, h))],
        out_specs=[pl.BlockSpec((rs, kvl), lambda h: (0, h)),
                   pl.BlockSpec((rs, 128), lambda h: (0, h))],
        out_shape=[jax.ShapeDtypeStruct((rs, nh * kvl), BF16),
                   jax.ShapeDtypeStruct((rs, nh * 128), BF16)],
        compiler_params=_params(("parallel",)),
        name="sample_q",
    )(q, prm["gk_n1"], prm["gk_r128"], prm["w_uk"])


SA_SLOTS = 8


def _make_sample_attn_kernel(nh, ds, kvl, page, n_pages, chunk_pages):
    nq = nh * ds
    nw = nh * QK_NOPE
    ct = chunk_pages * page
    nch = n_pages // chunk_pages
    inv_dim = 1.0 / QK_DIM

    new = SA_SLOTS
    pre = SA_SLOTS - 3
    assert nch % SA_SLOTS == 0 and nch % 2 == 0 and pre >= 3

    def kern(pt_ref, wt_ref, qa_ref, p2_ref, cnew_ref, rnew_ref, cc_ref, cr_ref, o_ref,
             l_ref, cbuf, rbuf, kt0, kt1, a20, a21, sem):
        n = pl.program_id(0)
        nseq = pl.num_programs(0)

        def copies(seq, j, slot):
            out = []
            for p in range(chunk_pages):
                pg = pt_ref[seq * n_pages + j * chunk_pages + p]
                out.append(pltpu.make_async_copy(cc_ref.at[pg], cbuf.at[slot, pl.ds(p * page, page)],
                                                 sem.at[slot, 2 * p]))
                out.append(pltpu.make_async_copy(cr_ref.at[pg], rbuf.at[slot, pl.ds(p * page, page)],
                                                 sem.at[slot, 2 * p + 1]))
            return out

        def request(t):
            seq2 = n + t // nch
            ch2 = t % nch

            @pl.when(seq2 < nseq)
            def _():
                for cp in copies(seq2, ch2, ch2 % SA_SLOTS):
                    cp.start()

        def arrive(j):
            @pl.when(j < nch)
            def _():
                for cp in copies(n, j, j % SA_SLOTS):
                    cp.wait()

        @pl.when(n == 0)
        def _():
            l_ref[0:nw, :] = wt_ref[...]
            for j in range(pre):
                for cp in copies(0, j, j):
                    cp.start()

        l_ref[nw:nw + nq, :] = qa_ref[0]
        cbuf[new] = jnp.zeros((ct, kvl), F32)
        rbuf[new] = jnp.zeros((ct, QK_ROPE), F32)
        cbuf[new, 0:8, :] = cnew_ref[0]
        rbuf[new, 0:8, :] = rnew_ref[0]

        def scores(j, kt_ref, a2_ref):
            slot = jnp.where(j < nch, j % SA_SLOTS, new)
            cb = cbuf[slot].astype(BF16)
            rb = rbuf[slot].astype(BF16)
            kt_ref[...] = lax.dot_general(l_ref[...], cb, _NT, preferred_element_type=F32)
            a2_ref[...] = lax.dot_general(p2_ref[0], rb, _NT, preferred_element_type=F32)

        def softmax_pv(slot, kt_ref, a2_ref, carry, causal):
            m, l, acc = carry
            kn = kt_ref[0:nw, :].reshape(nh, QK_NOPE, ct)
            ssq = jnp.sum(kn * kn, axis=1)
            kpt = a2_ref[nq:nq + QK_ROPE, :]
            ssq = ssq + jnp.sum(kpt * kpt, axis=0, keepdims=True)
            rf = lax.rsqrt(ssq * inv_dim + NORM_EPS)
            s = (kt_ref[nw:nw + nq, :] + a2_ref[0:nq, :]) * jnp.concatenate([rf] * ds, axis=0)
            if causal:
                qidx = lax.broadcasted_iota(jnp.int32, (nq, ct), 0) // nh
                kidx = lax.broadcasted_iota(jnp.int32, (nq, ct), 1)
                s = jnp.where(kidx <= qidx, s, NEG_INF)
            m_new = jnp.maximum(m, jnp.max(s, axis=1, keepdims=True))
            alpha = jnp.exp(m - m_new)
            p = jnp.exp(s - m_new)
            l = alpha * l + jnp.sum(p, axis=1, keepdims=True)
            acc = alpha * acc + jnp.dot(p.astype(BF16), cbuf[slot].astype(BF16), preferred_element_type=F32)
            return m_new, l, acc

        arrive(0)
        scores(0, kt0, a20)

        def body(i, carry):
            j = 2 * i
            request(j + pre)
            request(j + pre + 1)
            arrive(j + 1)
            arrive(j + 2)
            scores(j + 1, kt1, a21)
            carry = softmax_pv(j % SA_SLOTS, kt0, a20, carry, False)
            scores(j + 2, kt0, a20)
            return softmax_pv((j + 1) % SA_SLOTS, kt1, a21, carry, False)

        init = (jnp.full((nq, 1), NEG_INF, F32), jnp.zeros((nq, 1), F32), jnp.zeros((nq, kvl), F32))
        carry = lax.fori_loop(0, nch // 2, body, init)
        m, l, acc = softmax_pv(new, kt0, a20, carry, True)
        o_ref[0] = acc / l

    return kern


def _sample_attn(page_table, wt, qa, p2, cnew, rnew, cache_c, cache_r, nh, ds, chunk_pages=2):
    db, n_pages = page_table.shape
    _, page, kvl = cache_c.shape
    nq = nh * ds
    nw = nh * QK_NOPE
    ct = chunk_pages * page
    grid_spec = pltpu.PrefetchScalarGridSpec(
        num_scalar_prefetch=1,
        grid=(db,),
        in_specs=[pl.BlockSpec((nw, kvl), lambda n, pt: (0, 0)),
                  pl.BlockSpec((1, nq, kvl), lambda n, pt: (n, 0, 0)),
                  pl.BlockSpec((1, nq + QK_ROPE, QK_ROPE), lambda n, pt: (n, 0, 0)),
                  pl.BlockSpec((1, 8, kvl), lambda n, pt: (n, 0, 0)),
                  pl.BlockSpec((1, 8, QK_ROPE), lambda n, pt: (n, 0, 0)),
                  pl.BlockSpec(memory_space=pl.ANY),
                  pl.BlockSpec(memory_space=pl.ANY)],
        out_specs=pl.BlockSpec((1, nq, kvl), lambda n, pt: (n, 0, 0)),
        scratch_shapes=[pltpu.VMEM((nw + nq, kvl), BF16),
                        pltpu.VMEM((SA_SLOTS + 1, ct, kvl), F32),
                        pltpu.VMEM((SA_SLOTS + 1, ct, QK_ROPE), F32),
                        pltpu.VMEM((nw + nq, ct), F32), pltpu.VMEM((nw + nq, ct), F32),
                        pltpu.VMEM((nq + QK_ROPE, ct), F32), pltpu.VMEM((nq + QK_ROPE, ct), F32),
                        pltpu.SemaphoreType.DMA((SA_SLOTS, 2 * chunk_pages))],
    )
    return pl.pallas_call(
        _make_sample_attn_kernel(nh, ds, kvl, page, n_pages, chunk_pages),
        grid_spec=grid_spec,
        out_shape=jax.ShapeDtypeStruct((db, nq, kvl), F32),
        compiler_params=_params(("arbitrary",)),
        name="sample_attn",
    )(page_table.reshape(-1), wt, qa, p2, cnew, rnew, cache_c, cache_r)


def _sample_o_kernel(ol_ref, wuv_ref, o_ref):
    o_ref[...] = jnp.dot(ol_ref[...].astype(BF16), wuv_ref[...], preferred_element_type=F32).astype(BF16)


def _sample_o(olat, w_uv, nh):
    rs = olat.shape[0]
    kvl = w_uv.shape[0]
    return pl.pallas_call(
        _sample_o_kernel,
        grid=(nh,),
        in_specs=[pl.BlockSpec((rs, kvl), lambda h: (0, h)),
                  pl.BlockSpec((kvl, V_HEAD), lambda h: (0, h))],
        out_specs=pl.BlockSpec((rs, V_HEAD), lambda h: (0, h)),
        out_shape=jax.ShapeDtypeStruct((rs, nh * V_HEAD), BF16),
        compiler_params=_params(("parallel",)),
        name="sample_o",
    )(olat, w_uv)


def _merge_kernel(ya_ref, yb_ref, ga_ref, gb_ref, wa_ref, wb_ref, o_ref):
    pa = jnp.dot(ya_ref[...], wa_ref[...], preferred_element_type=F32)
    pb = jnp.dot(yb_ref[...], wb_ref[...], preferred_element_type=F32)
    o_ref[...] = (_sigmoid(ga_ref[...]) * pa + _sigmoid(gb_ref[...]) * pb).astype(BF16)


def _merge(ya, yb, cols, lay, wa, wb, tm=512, tn=512):
    r, d = ya.shape
    nj = d // tn
    ga0 = lay["cb"]["gate_a"] * (d // tn)
    gb0 = lay["cb"]["gate_b"] * (d // tn)
    return pl.pallas_call(
        _merge_kernel,
        grid=(nj, r // tm),
        in_specs=[pl.BlockSpec((tm, d), lambda j, i: (i, 0)),
                  pl.BlockSpec((tm, d), lambda j, i: (i, 0)),
                  pl.BlockSpec((tm, tn), lambda j, i: (i, ga0 + j)),
                  pl.BlockSpec((tm, tn), lambda j, i: (i, gb0 + j)),
                  pl.BlockSpec((d, tn), lambda j, i: (0, j)),
                  pl.BlockSpec((d, tn), lambda j, i: (0, j))],
        out_specs=pl.BlockSpec((tm, tn), lambda j, i: (i, j)),
        out_shape=jax.ShapeDtypeStruct((r, d), BF16),
        compiler_params=_params(("parallel", "parallel")),
        name="branch_merge",
    )(ya, yb, cols, cols, wa, wb)


def _make_out_router_kernel(n_experts):
    def kern(m_ref, x_ref, wo_ref, g2_ref, wrh_ref, wrl_ref, br_ref, x1_ref, h_ref, gate_ref, idx_ref):
        x1 = x_ref[...] + jnp.dot(m_ref[...], wo_ref[...], preferred_element_type=F32)
        x1_ref[...] = x1
        h = x1 * lax.rsqrt(jnp.mean(x1 * x1, axis=-1, keepdims=True) + NORM_EPS) * g2_ref[...]
        h_ref[...] = h
        hh, hl = _split(h)
        logits = (jnp.dot(hh, wrh_ref[...], preferred_element_type=F32)
                  + jnp.dot(hh, wrl_ref[...], preferred_element_type=F32)
                  + jnp.dot(hl, wrh_ref[...], preferred_element_type=F32)) + br_ref[...]
        lane = lax.broadcasted_iota(jnp.int32, logits.shape, 1)
        work = jnp.where(lane < n_experts, logits, -jnp.inf)
        vals, idxs = [], []
        for _ in range(TOP_K):
            mx = jnp.max(work, axis=1, keepdims=True)
            ix = jnp.min(jnp.where(work == mx, lane, 128), axis=1, keepdims=True)
            vals.append(mx)
            idxs.append(ix)
            work = jnp.where(lane == ix, -jnp.inf, work)
        ex = [jnp.exp(v_ - vals[0]) for v_ in vals]
        den = ex[0] + ex[1] + ex[2] + ex[3]
        gate = jnp.zeros(logits.shape, F32)
        idx = jnp.zeros(logits.shape, jnp.int32)
        for k_ in range(TOP_K):
            gate = jnp.where(lane == k_, ex[k_] / den, gate)
            idx = jnp.where(lane == k_, idxs[k_], idx)
        gate_ref[...] = gate
        idx_ref[...] = idx

    return kern


def _out_router(merged, x, wo, ln2, wr_hi, wr_lo, br, n_experts, tm=256):
    r, d = x.shape
    row = pl.BlockSpec((tm, d), lambda i: (i, 0))
    small = pl.BlockSpec((tm, 128), lambda i: (i, 0))
    full = lambda a_: pl.BlockSpec(a_.shape, lambda i: (0, 0))
    return pl.pallas_call(
        _make_out_router_kernel(n_experts),
        grid=(r // tm,),
        in_specs=[row, row, full(wo), full(ln2), full(wr_hi), full(wr_lo), full(br)],
        out_specs=[row, row, small, small],
        out_shape=[jax.ShapeDtypeStruct((r, d), F32), jax.ShapeDtypeStruct((r, d), F32),
                   jax.ShapeDtypeStruct((r, 128), F32), jax.ShapeDtypeStruct((r, 128), jnp.int32)],
        compiler_params=_params(("parallel",)),
        name="out_proj_router",
    )(merged, x, wo, ln2, wr_hi, wr_lo, br)


def _make_gather_kernel(bm):
    def kern(nused_ref, tok_ref, h_ref, o_ref, buf, sem):
        b = pl.program_id(0)

        @pl.when(b < nused_ref[0])
        def _():
            def issue(i, carry):
                pltpu.make_async_copy(h_ref.at[pl.ds(tok_ref[0, 0, i], 1)], buf.at[pl.ds(i, 1)], sem).start()
                return carry

            lax.fori_loop(0, bm, issue, 0)
            pltpu.make_async_copy(h_ref.at[pl.ds(0, bm)], buf, sem).wait()
            o_ref[...] = buf[...].astype(BF16)

        @pl.when(b >= nused_ref[0])
        def _():
            o_ref[...] = jnp.zeros(o_ref.shape, BF16)

    return kern


def _moe_gather(h, tok_sorted, nused, bm):
    mp = tok_sorted.shape[0]
    d = h.shape[1]
    nb = mp // bm
    grid_spec = pltpu.PrefetchScalarGridSpec(
        num_scalar_prefetch=1,
        grid=(nb,),
        in_specs=[pl.BlockSpec((1, 1, bm), lambda b, nu: (b, 0, 0), memory_space=pltpu.SMEM),
                  pl.BlockSpec(memory_space=pl.ANY)],
        out_specs=pl.BlockSpec((bm, d), lambda b, nu: (b, 0)),
        scratch_shapes=[pltpu.VMEM((bm, d), F32), pltpu.SemaphoreType.DMA(())],
    )
    return pl.pallas_call(
        _make_gather_kernel(bm),
        grid_spec=grid_spec,
        out_shape=jax.ShapeDtypeStruct((mp, d), BF16),
        compiler_params=_params(("arbitrary",)),
        name="moe_gather",
    )(nused, tok_sorted.reshape(nb, 1, bm), h)


WK_E, WK_J, WK_B, WK_JO, WK_VALID, WK_FIRST = range(6)


def _moe_up_kernel(wk_ref, x_ref, wg_ref, wu_ref, bg_ref, bu_ref, o_ref, wgb, wub):
    w = pl.program_id(0)

    @pl.when(wk_ref[WK_VALID, w] == 1)
    def _():
        @pl.when(wk_ref[WK_FIRST, w] == 1)
        def _():
            wgb[...] = wg_ref[0].astype(BF16)
            wub[...] = wu_ref[0].astype(BF16)

        x = x_ref[...]
        hg = jnp.dot(x, wgb[...], preferred_element_type=F32) + bg_ref[0]
        hu = jnp.dot(x, wub[...], preferred_element_type=F32) + bu_ref[0]
        gt = jnp.minimum(hg, SWIGLU_LIMIT)
        up = jnp.clip(hu, -SWIGLU_LIMIT, SWIGLU_LIMIT)
        o_ref[...] = ((up + 1.0) * gt * _sigmoid(SWIGLU_ALPHA * gt)).astype(BF16)

    @pl.when(wk_ref[WK_VALID, w] == 0)
    def _():
        o_ref[...] = jnp.zeros(o_ref.shape, BF16)


def _moe_up(xs, w_gu, b_gu, work, bm, tf):
    mp, d = xs.shape
    ne, _, f2 = w_gu.shape
    dff = f2 // 2
    nj = dff // tf
    nwork = work.shape[1]
    grid_spec = pltpu.PrefetchScalarGridSpec(
        num_scalar_prefetch=1,
        grid=(nwork,),
        in_specs=[pl.BlockSpec((bm, d), lambda w, wk: (wk[WK_B, w], 0)),
                  pl.BlockSpec((1, d, tf), lambda w, wk: (wk[WK_E, w], 0, wk[WK_J, w])),
                  pl.BlockSpec((1, d, tf), lambda w, wk: (wk[WK_E, w], 0, nj + wk[WK_J, w])),
                  pl.BlockSpec((1, 1, tf), lambda w, wk: (wk[WK_E, w], 0, wk[WK_J, w])),
                  pl.BlockSpec((1, 1, tf), lambda w, wk: (wk[WK_E, w], 0, nj + wk[WK_J, w]))],
        out_specs=pl.BlockSpec((bm, tf), lambda w, wk: (wk[WK_B, w], wk[WK_JO, w])),
        scratch_shapes=[pltpu.VMEM((d, tf), BF16), pltpu.VMEM((d, tf), BF16)],
    )
    return pl.pallas_call(
        _moe_up_kernel,
        grid_spec=grid_spec,
        out_shape=jax.ShapeDtypeStruct((mp, dff), BF16),
        compiler_params=_params(("arbitrary",)),
        name="moe_up",
    )(work, xs, w_gu, w_gu, b_gu, b_gu)


def _moe_down_kernel(wk_ref, a_ref, wd_ref, bd_ref, gt_ref, o_ref, wdb):
    w = pl.program_id(0)

    @pl.when(wk_ref[WK_VALID, w] == 1)
    def _():
        @pl.when(wk_ref[WK_FIRST, w] == 1)
        def _():
            wdb[...] = wd_ref[0].astype(BF16)

        y = jnp.dot(a_ref[...], wdb[...], preferred_element_type=F32) + bd_ref[0]
        o_ref[...] = y * gt_ref[...]

    @pl.when(wk_ref[WK_VALID, w] == 0)
    def _():
        o_ref[...] = jnp.zeros(o_ref.shape, F32)


def _moe_down(act, w_down, b_down, gate_sorted, work, bm, tn):
    mp, dff = act.shape
    ne, _, d = w_down.shape
    nwork = work.shape[1]
    grid_spec = pltpu.PrefetchScalarGridSpec(
        num_scalar_prefetch=1,
        grid=(nwork,),
        in_specs=[pl.BlockSpec((bm, dff), lambda w, wk: (wk[WK_B, w], 0)),
                  pl.BlockSpec((1, dff, tn), lambda w, wk: (wk[WK_E, w], 0, wk[WK_J, w])),
                  pl.BlockSpec((1, 1, tn), lambda w, wk: (wk[WK_E, w], 0, wk[WK_J, w])),
                  pl.BlockSpec((bm, 1), lambda w, wk: (wk[WK_B, w], 0))],
        out_specs=pl.BlockSpec((bm, tn), lambda w, wk: (wk[WK_B, w], wk[WK_JO, w])),
        scratch_shapes=[pltpu.VMEM((dff, tn), BF16)],
    )
    return pl.pallas_call(
        _moe_down_kernel,
        grid_spec=grid_spec,
        out_shape=jax.ShapeDtypeStruct((mp, d), F32),
        compiler_params=_params(("arbitrary",)),
        name="moe_down",
    )(work, act, w_down, b_down, gate_sorted)


def _make_combine_kernel(tm):
    def kern(pos_ref, x_ref, ys_ref, o_ref, buf, sem):
        def issue(i, carry):
            for k_ in range(TOP_K):
                pltpu.make_async_copy(ys_ref.at[pl.ds(pos_ref[0, 0, i * TOP_K + k_], 1)],
                                      buf.at[k_, pl.ds(i, 1)], sem).start()
            return carry

        lax.fori_loop(0, tm, issue, 0)
        for k_ in range(TOP_K):
            pltpu.make_async_copy(ys_ref.at[pl.ds(0, tm)], buf.at[k_], sem).wait()
        o_ref[...] = x_ref[...] + ((buf[0] + buf[1]) + (buf[2] + buf[3]))

    return kern


def _moe_combine(x1, ys, pos, tm=128):
    r, d = x1.shape
    return pl.pallas_call(
        _make_combine_kernel(tm),
        grid=(r // tm,),
        in_specs=[pl.BlockSpec((1, 1, tm * TOP_K), lambda i: (i, 0, 0), memory_space=pltpu.SMEM),
                  pl.BlockSpec((tm, d), lambda i: (i, 0)),
                  pl.BlockSpec(memory_space=pl.ANY)],
        out_specs=pl.BlockSpec((tm, d), lambda i: (i, 0)),
        out_shape=jax.ShapeDtypeStruct((r, d), F32),
        scratch_shapes=[pltpu.VMEM((TOP_K, tm, d), F32), pltpu.SemaphoreType.DMA(())],
        compiler_params=_params(("arbitrary",)),
        name="moe_combine",
    )(pos.reshape(r // tm, 1, tm * TOP_K), x1, ys)


def _moe_routing(idx, gate, real_rows, n_rows, n_experts, bm, nj):
    n_real = real_rows.shape[0]
    m = n_real * TOP_K
    nb_total = -(-m // bm) + n_experts
    mp = nb_total * bm
    e_flat = idx[real_rows].reshape(m)
    g_flat = gate[real_rows].reshape(m)
    tok_flat = jnp.repeat(jnp.asarray(real_rows, jnp.int32), TOP_K)
    order = jnp.argsort(e_flat)
    e_sorted = e_flat[order]
    counts = jnp.zeros((n_experts,), jnp.int32).at[e_flat].add(1)
    starts = jnp.cumsum(counts) - counts
    nblk = (counts + bm - 1) // bm
    blk_ends = jnp.cumsum(nblk)
    blk_starts = blk_ends - nblk
    slot = blk_starts[e_sorted] * bm + (jnp.arange(m, dtype=jnp.int32) - starts[e_sorted])
    tok_sorted = jnp.zeros((mp,), jnp.int32).at[slot].set(tok_flat[order])
    gate_sorted = jnp.zeros((mp,), F32).at[slot].set(g_flat[order])
    pos_flat = jnp.zeros((m,), jnp.int32).at[order].set(slot)
    pos = jnp.zeros((n_rows, TOP_K), jnp.int32).at[real_rows].set(pos_flat.reshape(n_real, TOP_K))
    nused = blk_ends[-1]
    w = jnp.arange(nb_total * nj, dtype=jnp.int32)
    valid = w < nused * nj
    wc = jnp.minimum(w, jnp.maximum(nused * nj - 1, 0))
    blk_e = jnp.minimum(jnp.searchsorted(blk_ends, jnp.arange(nb_total, dtype=jnp.int32), side="right"),
                        n_experts - 1).astype(jnp.int32)
    we = blk_e[wc // nj]
    local = wc - nj * blk_starts[we]
    nbe = jnp.maximum(nblk[we], 1)
    wj = local // nbe
    wf = ((local % nbe) == 0) & valid
    wb = jnp.where(valid, blk_starts[we] + local % nbe, w // nj)
    wjo = jnp.where(valid, wj, w % nj)
    work = jnp.stack([we, wj, wb, wjo, valid.astype(jnp.int32), wf.astype(jnp.int32)]).astype(jnp.int32)
    return tok_sorted, gate_sorted, pos, nused.reshape(1).astype(jnp.int32), work


def _layout(d, nh_mla, kvl, n_w, n_a, n_g):
    assert d == 2048 and nh_mla * QK_NOPE == d and kvl == 512 and n_g == 256 and n_w <= 128 and n_a <= 128
    cb = {"q_nope": 0, "q_r1": 4, "q_r2": 5, "c": 6, "kpe": 28, "gate_a": 2, "gate_b": 3,
          "r": 4, "k": 5, "v": 6, "lora": 28}
    return {"d": d, "mla_heads": nh_mla, "cb": cb, "nc": 14848}


def _relayout_w_in(w, d, nh, kvl, n_w, n_a, n_g):
    o_w = 3 * d
    o_a = o_w + n_w
    o_g = o_a + n_a
    c1 = o_g + n_g
    c2 = c1 + nh * QK_DIM
    c3 = c2 + kvl
    c4 = c3 + QK_ROPE
    q = w[:, c1:c2].reshape(d, nh, QK_DIM)
    z = lambda n_: jnp.zeros((d, n_), w.dtype)
    half = QK_ROPE // 2
    parts = [q[:, :, :QK_NOPE].reshape(d, nh * QK_NOPE),
             q[:, :, QK_NOPE:QK_NOPE + half].reshape(d, nh * half),
             q[:, :, QK_NOPE + half:].reshape(d, nh * half),
             w[:, c2:c3], w[:, c3:c4], z(128 - QK_ROPE), z(384),
             w[:, c4:c4 + d], w[:, c4 + d:c4 + 2 * d],
             w[:, 0:3 * d],
             w[:, o_w:o_a], z(128 - n_w), w[:, o_a:o_g], z(128 - n_a), w[:, o_g:c1]]
    return jnp.concatenate(parts, axis=1).astype(BF16)


def _pad_lora_vec(vec, d, n_w, n_a, n_g):
    o_w = 3 * d
    o_a = o_w + n_w
    o_g = o_a + n_a
    z = lambda n_: jnp.zeros(vec.shape[:-1] + (n_,), vec.dtype)
    lora = jnp.concatenate([vec[..., o_w:o_a], z(128 - n_w), vec[..., o_a:o_g], z(128 - n_a),
                            vec[..., o_g:o_g + n_g]], axis=-1)
    return vec[..., 0:d], vec[..., d:2 * d], vec[..., 2 * d:3 * d], lora


def _unpad_shift(cols_row, lay, d, n_w, n_a, n_g):
    rkv = cols_row[..., 4 * d:7 * d]
    lo = cols_row[..., 7 * d:7 * d + 512]
    return jnp.concatenate([rkv, lo[..., 0:n_w], lo[..., 128:128 + n_a], lo[..., 256:256 + n_g]], axis=-1)


def _rope_tables(pos, nh):
    half = QK_ROPE // 2
    inv = ROPE_THETA ** (-jnp.arange(half, dtype=F32) / half)
    ang = pos.astype(F32)[:, None] * inv[None, :]
    cos, sin = jnp.cos(ang), jnp.sin(ang)
    z = jnp.zeros((pos.shape[0], 128 - QK_ROPE), F32)
    return {"cos": jnp.tile(cos, (1, nh)), "sin": jnp.tile(sin, (1, nh)),
            "c128": jnp.concatenate([cos, cos, z], axis=1),
            "s128": jnp.concatenate([-sin, sin, z], axis=1)}


def kernel(x_prompt, x_sample, cache_kv_latent, cache_k_rope, state_wkv, state_shift, page_table, meta_tokens,
           ln1_g, w_in, rw_mu, rw_w0, rw_w2, rw_a0, rw_a2, rw_g2, rw_k_k, rw_k_a, rw_r_k, rw_ln_g, rw_ln_b,
           q_norm_g, k_norm_g, kv_norm_g, w_kv_up, w_branch_a, w_branch_b, w_out, ln2_g,
           w_router, b_router, w_gu, b_gu, w_down, b_down):
    depth = w_in.shape[0]
    nb, seq, d = x_prompt.shape
    db, ds, _ = x_sample.shape
    n_meta = meta_tokens.shape[0]
    t = seq + n_meta
    tp = -(-t // ATTN_TILE) * ATTN_TILE
    rp = nb * tp
    rs = db * ds
    r_all = rp + rs
    n_pages = page_table.shape[1]
    page = cache_kv_latent.shape[2]
    kvl = cache_kv_latent.shape[3]
    past = n_pages * page
    nh_rw = d // RWKV_HEAD
    n_w, n_a, n_g = rw_w2.shape[1], rw_a2.shape[1], rw_g2.shape[1]
    nh = w_kv_up.shape[2] // (QK_NOPE + V_HEAD)
    n_experts = w_router.shape[2]
    lay = _layout(d, nh, kvl, n_w, n_a, n_g)
    assert r_all % 1024 == 0 and rs % 256 == 0 and ds <= 8

    meta = jnp.broadcast_to(meta_tokens[None].astype(x_prompt.dtype), (nb, n_meta, d))
    xp = jnp.concatenate([meta, x_prompt, jnp.zeros((nb, tp - t, d), x_prompt.dtype)], axis=1)
    x = jnp.concatenate([xp.reshape(rp, d), x_sample.reshape(rs, d)], axis=0)

    pos_all = jnp.concatenate([jnp.tile(jnp.arange(tp), nb), jnp.tile(past + jnp.arange(ds), db)])
    tabs = _rope_tables(pos_all, nh)
    real_rows = np.concatenate([np.arange(t) + b_ * tp for b_ in range(nb)] + [rp + np.arange(rs)]).astype(np.int32)

    outs = {k_: [] for k_ in ("lat_p", "kpe_p", "wkv_p", "sh_p", "lat_s", "kpe_s", "wkv_s", "sh_s")}
    for l in range(depth):
        w_in_l = _relayout_w_in(w_in[l], d, nh, kvl, n_w, n_a, n_g)
        cols = _in_proj(x, ln1_g[l][None], w_in_l)

        mu_r, mu_k, mu_v, mu_l = _pad_lora_vec(rw_mu[l][None], d, n_w, n_a, n_g)
        zrow = lambda a_, n_: jnp.concatenate([a_, jnp.zeros((128 - n_, d), a_.dtype)], axis=0)
        rprm = {"mu_r": mu_r, "mu_k": mu_k, "mu_v": mu_v, "mu_l": mu_l,
                "w0": rw_w0[l][None], "w2": zrow(rw_w2[l], n_w), "a0": rw_a0[l][None], "a2": zrow(rw_a2[l], n_a),
                "g2": rw_g2[l], "k_k": rw_k_k[l][None], "k_a": rw_k_a[l][None],
                "ln_g": rw_ln_g[l][None], "ln_b": rw_ln_b[l][None], "r_k": rw_r_k[l].reshape(1, d)}
        pp = _rwkv_prep(cols, lay, 0, rp, False, None, rprm, tp // 128, t, ds)
        sh = jnp.zeros((db, ds, state_shift.shape[2]), F32).at[:, 0].set(state_shift[l]).reshape(rs, -1)
        sp = _rwkv_prep(cols, lay, rp, rs, True, _pad_lora_vec(sh, d, n_w, n_a, n_g), rprm, 1, ds, ds)
        r_p, lw_p, k_p, v_p, kk_p, b_p, g_p = pp
        r_s, lw_s, k_s, v_s, kk_s, b_s, g_s = sp
        y_p, wkv_p = _wkv_prompt([r_p, lw_p, k_p, v_p, kk_p, b_p], nb, tp, t)
        seq_s = [jnp.pad(a_.reshape(db, ds, d), ((0, 0), (0, 8 - ds), (0, 0))).reshape(db * 8, d)
                 for a_ in (r_s, lw_s, k_s, v_s, kk_s, b_s)]
        y_s, wkv_s = _wkv_sample(seq_s, state_wkv[l], 8)
        y_s = y_s.reshape(db, 8, d)[:, :ds].reshape(rs, d)
        ya = jnp.concatenate([_rwkv_post(y_p, r_p, k_p, v_p, g_p, rprm),
                              _rwkv_post(y_s, r_s, k_s, v_s, g_s, rprm)], axis=0)

        w_up = w_kv_up[l].reshape(kvl, nh, QK_NOPE + V_HEAD)
        gq, gk = q_norm_g[l], k_norm_g[l]
        half = QK_ROPE // 2
        z64 = jnp.zeros((128 - QK_ROPE,), F32)
        mprm = {"gq_n": jnp.tile(gq[:QK_NOPE], nh)[None], "gq_r": jnp.tile(gq[QK_NOPE:], nh)[None],
                "gk_n": jnp.tile(gk[:QK_NOPE], nh)[None], "gk_n1": gk[:QK_NOPE][None],
                "gk_r128": jnp.concatenate([gk[QK_NOPE:], gk[QK_NOPE:], z64])[None],
                "kv_g": kv_norm_g[l][None],
                "w_uk": w_up[:, :, :QK_NOPE].reshape(kvl, nh * QK_NOPE).astype(BF16),
                "w_uv": w_up[:, :, QK_NOPE:].reshape(kvl, nh * V_HEAD).astype(BF16)}
        q_p, c_p, kpe_p, kx_p, vx_p = _mla_prep(cols, lay, 0, rp, True, tabs, mprm)
        q_s, c_s, kpe_s = _mla_prep(cols, lay, rp, rs, False, tabs, mprm)
        o_p = _prompt_attn(q_p, kx_p, vx_p, nb, tp, nh)

        qa, qr = _sample_q(q_s, mprm, nh)
        qa = qa.reshape(db, ds * nh, kvl)
        qr = qr.reshape(db, ds, nh, 128)[..., :QK_ROPE].reshape(db, ds * nh, QK_ROPE)
        p2 = jnp.concatenate([qr, jnp.broadcast_to(jnp.eye(QK_ROPE, dtype=BF16)[None], (db, QK_ROPE, QK_ROPE))], axis=1)
        pad8 = lambda a_: jnp.concatenate([a_, jnp.zeros((db, 8 - ds, a_.shape[-1]), F32)], axis=1)
        olat = _sample_attn(page_table, mprm["w_uk"].T, qa, p2, pad8(c_s.reshape(db, ds, kvl)),
                            pad8(kpe_s.reshape(db, ds, QK_ROPE)), cache_kv_latent[l], cache_k_rope[l], nh, ds)
        o_s = _sample_o(olat.reshape(rs, nh * kvl), mprm["w_uv"], nh)
        yb = jnp.concatenate([o_p, o_s], axis=0)

        merged = _merge(ya, yb, cols, lay, w_branch_a[l].astype(BF16), w_branch_b[l].astype(BF16))
        wr = jnp.concatenate([w_router[l], jnp.zeros((d, 128 - n_experts), F32)], axis=1)
        wr_hi = wr.astype(BF16)
        wr_lo = (wr - wr_hi.astype(F32)).astype(BF16)
        br = jnp.concatenate([b_router[l], jnp.zeros((128 - n_experts,), F32)])[None]
        x1, hmoe, gate, idx = _out_router(merged, x, w_out[l].astype(BF16), ln2_g[l][None], wr_hi, wr_lo, br, n_experts)

        dff = w_down.shape[2]
        tok_sorted, gate_sorted, pos, nused, work_up = _moe_routing(
            idx[:, :TOP_K], gate[:, :TOP_K], real_rows, r_all, n_experts, MOE_BM, dff // MOE_TF)
        xs = _moe_gather(hmoe, tok_sorted, nused, MOE_BM)
        act = _moe_up(xs, w_gu[l], b_gu[l][:, None, :], work_up, MOE_BM, MOE_TF)
        work_dn = work_up if d // MOE_TN == dff // MOE_TF else _moe_routing(
            idx[:, :TOP_K], gate[:, :TOP_K], real_rows, r_all, n_experts, MOE_BM, d // MOE_TN)[4]
        ys = _moe_down(act, w_down[l], b_down[l][:, None, :], gate_sorted[:, None], work_dn, MOE_BM, MOE_TN)
        x = _moe_combine(x1, ys, pos)

        outs["lat_p"].append(c_p.reshape(nb, tp, kvl)[:, :t])
        outs["kpe_p"].append(kpe_p.reshape(nb, tp, QK_ROPE)[:, :t])
        outs["wkv_p"].append(wkv_p)
        last_p = np.arange(nb, dtype=np.int32) * tp + (t - 1)
        last_s = rp + np.arange(db, dtype=np.int32) * ds + (ds - 1)
        outs["sh_p"].append(_unpad_shift(cols[last_p], lay, d, n_w, n_a, n_g))
        outs["lat_s"].append(c_s.reshape(db, ds, kvl))
        outs["kpe_s"].append(kpe_s.reshape(db, ds, QK_ROPE))
        outs["wkv_s"].append(wkv_s)
        outs["sh_s"].append(_unpad_shift(cols[last_s], lay, d, n_w, n_a, n_g))

    y_prompt = x[:rp].reshape(nb, tp, d)[:, n_meta:t]
    y_sample = x[rp:].reshape(db, ds, d)
    st = lambda k_: jnp.stack(outs[k_])
    return (y_prompt, y_sample, st("lat_p"), st("kpe_p"), st("wkv_p"), st("sh_p"),
            st("lat_s"), st("kpe_s"), st("wkv_s"), st("sh_s"))
```

```python
import functools

import numpy as np
import jax
import jax.numpy as jnp
from jax import lax
from jax.experimental import pallas as pl
from jax.experimental.pallas import tpu as pltpu

F32 = jnp.float32
BF16 = jnp.bfloat16

RWKV_HEAD = 64
QK_NOPE = 128
QK_ROPE = 64
QK_DIM = QK_NOPE + QK_ROPE
V_HEAD = 128
QK_PAD = 256
ROPE_THETA = 10000.0
RWKV_GN_EPS = 64e-5
NORM_EPS = 1e-6
NEG_INF = -1e30
TOP_K = 4
SWIGLU_LIMIT = 7.0
SWIGLU_ALPHA = 1.702

VMEM_LIMIT = 56 * 1024 * 1024

ATTN_TILE = 256
MOE_BM = 256
MOE_TF = 1024
MOE_TN = 1024


def _params(sem):
    return pltpu.CompilerParams(dimension_semantics=sem, vmem_limit_bytes=VMEM_LIMIT)


def _split(x):
    hi = x.astype(BF16)
    lo = (x - hi.astype(F32)).astype(BF16)
    return hi, lo


def _dot2(x, m):
    hi, lo = _split(x)
    return (jnp.dot(hi, m, preferred_element_type=F32) + jnp.dot(lo, m, preferred_element_type=F32))


def _sigmoid(x):
    return 1.0 / (1.0 + jnp.exp(-x))


def _seg_matrix(n, seg):
    e = (np.arange(n)[:, None] // seg == np.arange(n // seg)[None, :]).astype(np.float32)
    return jnp.asarray(e, BF16), jnp.asarray(e.T, BF16)


def _in_proj_kernel(x_ref, g_ref, w_ref, o_ref, xn_ref):
    @pl.when(pl.program_id(1) == 0)
    def _():
        x = x_ref[...]
        ms = jnp.mean(x * x, axis=-1, keepdims=True)
        xn_ref[...] = (x * lax.rsqrt(ms + NORM_EPS) * g_ref[...]).astype(BF16)

    o_ref[...] = jnp.dot(xn_ref[...], w_ref[...], preferred_element_type=F32)


def _in_proj(x, g, w, tm=1024, tn=512):
    r, d = x.shape
    nc = w.shape[1]
    return pl.pallas_call(
        _in_proj_kernel,
        grid=(r // tm, nc // tn),
        in_specs=[pl.BlockSpec((tm, d), lambda i, j: (i, 0)),
                  pl.BlockSpec((1, d), lambda i, j: (0, 0)),
                  pl.BlockSpec((d, tn), lambda i, j: (0, j))],
        out_specs=pl.BlockSpec((tm, tn), lambda i, j: (i, j)),
        out_shape=jax.ShapeDtypeStruct((r, nc), F32),
        scratch_shapes=[pltpu.VMEM((tm, d), BF16)],
        compiler_params=_params(("parallel", "arbitrary")),
        name="in_proj",
    )(x, g, w)


def _make_rwkv_prep_kernel(sample, tr, tiles_per_seq, t_real, ds):
    def kern(r_ref, k_ref, v_ref, l_ref, xr_ref, xk_ref, xv_ref, xl_ref,
             mur_ref, muk_ref, muv_ref, mul_ref, w0_ref, w2_ref, a0_ref, a2_ref, g2_ref,
             kk_ref, ka_ref, e_ref, et_ref,
             ro_ref, lwo_ref, ko_ref, vo_ref, kko_ref, bo_ref, go_ref):
        i = pl.program_id(0)

        def mixed(x_ref, extra_ref, mu_ref):
            x = x_ref[...]
            rolled = pltpu.roll(x, 1, axis=0)
            row = lax.broadcasted_iota(jnp.int32, x.shape, 0)
            if sample:
                prev = jnp.where(row % ds == 0, extra_ref[...], rolled)
            else:
                halo = jnp.where((i % tiles_per_seq) == 0, 0.0, extra_ref[7:8, :])
                prev = jnp.where(row == 0, halo, rolled)
            return x + mu_ref[...] * (prev - x)

        r = mixed(r_ref, xr_ref, mur_ref)
        k = mixed(k_ref, xk_ref, muk_ref)
        v = mixed(v_ref, xv_ref, muv_ref)
        lo = mixed(l_ref, xl_ref, mul_ref)
        w_in = lo[:, 0:128]
        a_in = lo[:, 128:256]
        g_in = lo[:, 256:512]
        z = w0_ref[...] + jnp.dot(jnp.tanh(w_in), w2_ref[...], preferred_element_type=F32)
        nz = -z
        softplus = jnp.maximum(nz, 0.0) + jnp.log(1.0 + jnp.exp(-jnp.abs(nz)))
        lw = -jnp.exp(-softplus - 0.5)
        a = _sigmoid(a0_ref[...] + jnp.dot(a_in, a2_ref[...], preferred_element_type=F32))
        g = jnp.dot(_sigmoid(g_in), g2_ref[...], preferred_element_type=F32)
        kk = k * kk_ref[...]
        ssq = _dot2(_dot2(kk * kk, e_ref[...]), et_ref[...])
        kk = kk * lax.rsqrt(jnp.maximum(ssq, 1e-24))
        k2 = k * (1.0 + (a - 1.0) * ka_ref[...])
        b = kk * a
        if not sample:
            row = lax.broadcasted_iota(jnp.int32, (tr, 1), 0) + (i % tiles_per_seq) * tr
            valid = row < t_real
            lw = jnp.where(valid, lw, 0.0)
            k2 = jnp.where(valid, k2, 0.0)
            kk = jnp.where(valid, kk, 0.0)
            b = jnp.where(valid, b, 0.0)
        ro_ref[...] = r
        lwo_ref[...] = lw
        ko_ref[...] = k2
        vo_ref[...] = v
        kko_ref[...] = kk
        bo_ref[...] = b
        go_ref[...] = g

    return kern


def _rwkv_prep(cols, lay, row0, nrows, sample, extras, prm, tiles_per_seq, t_real, ds, tr=128):
    d = lay["d"]
    rb0 = row0 // tr
    cb = lay["cb"]
    e64, et64 = _seg_matrix(d, RWKV_HEAD)

    def colspec(width, blk):
        return pl.BlockSpec((tr, width), lambda i: (rb0 + i, blk))

    in_specs = [colspec(d, cb["r"]), colspec(d, cb["k"]), colspec(d, cb["v"]), colspec(512, cb["lora"])]
    args = [cols, cols, cols, cols]
    if sample:
        for a_, w_ in zip(extras, (d, d, d, 512)):
            in_specs.append(pl.BlockSpec((tr, w_), lambda i: (i, 0)))
            args.append(a_)
    else:
        def halospec(width, blk):
            return pl.BlockSpec((8, width), lambda i: (jnp.maximum((rb0 + i) * (tr // 8) - 1, 0), blk))
        in_specs += [halospec(d, cb["r"]), halospec(d, cb["k"]), halospec(d, cb["v"]), halospec(512, cb["lora"])]
        args += [cols, cols, cols, cols]
    for name in ("mu_r", "mu_k", "mu_v", "mu_l", "w0", "w2", "a0", "a2", "g2", "k_k", "k_a"):
        a_ = prm[name]
        in_specs.append(pl.BlockSpec(a_.shape, lambda i: (0, 0)))
        args.append(a_)
    in_specs += [pl.BlockSpec(e64.shape, lambda i: (0, 0)), pl.BlockSpec(et64.shape, lambda i: (0, 0))]
    args += [e64, et64]
    out = jax.ShapeDtypeStruct((nrows, d), F32)
    return pl.pallas_call(
        _make_rwkv_prep_kernel(sample, tr, tiles_per_seq, t_real, ds),
        grid=(nrows // tr,),
        in_specs=in_specs,
        out_specs=[pl.BlockSpec((tr, d), lambda i: (i, 0))] * 7,
        out_shape=[out] * 7,
        compiler_params=_params(("parallel",)),
        name="rwkv_prep_sample" if sample else "rwkv_prep_prompt",
    )(*args)


_NT = (((1,), (1,)), ((), ()))
_TN = (((0,), (0,)), ((), ()))
PAIR = 2 * RWKV_HEAD


def _wkv_masks(c):
    rt = lax.broadcasted_iota(jnp.int32, (2 * c, 2 * c), 0)
    ct = lax.broadcasted_iota(jnp.int32, (2 * c, 2 * c), 1)
    same = (rt // c) == (ct // c)
    strict = same & ((ct % c) < (rt % c))
    incl = same & ((ct % c) <= (rt % c))
    lane = lax.broadcasted_iota(jnp.int32, (c, PAIR), 1)
    return strict, incl, lane < RWKV_HEAD


def _wkv_pair_chunks(items, masks, c):
    strict, incl, m0 = masks
    n = range(len(items))
    c2 = 2 * c
    dot = functools.partial(jnp.dot, preferred_element_type=F32)
    b16 = lambda x: x.astype(BF16)

    def ext(x):
        return jnp.concatenate([jnp.where(m0, x, 0.0), jnp.where(m0, 0.0, x)], axis=0).astype(BF16)

    qr = [jnp.concatenate([ext(it[2]), ext(it[1])], axis=0) for it in items]
    nbk = [jnp.concatenate([ext(-it[3]), ext(it[4])], axis=0) for it in items]
    vx = [ext(it[5]) for it in items]
    a = [lax.dot_general(qr[i], nbk[i], _NT, preferred_element_type=F32) for i in n]
    qg = [lax.dot_general(qr[i], b16(items[i][0]), _NT, preferred_element_type=F32) for i in n]
    lb = [jnp.where(strict, -a[i][0:c2, 0:c2], 0.0) for i in n]
    lk = [b16(jnp.where(strict, a[i][0:c2, c2:2 * c2], 0.0)) for i in n]
    mbk = [b16(jnp.concatenate([jnp.where(incl, a[i][c2:2 * c2, 0:c2], 0.0),
                                jnp.where(incl, a[i][c2:2 * c2, c2:2 * c2], 0.0)], axis=1)) for i in n]
    lp = [b16(x) for x in lb]
    u = [qg[i][0:c2] + dot(lk[i], vx[i]) for i in n]
    sign = -1.0
    span = 1
    while span < c:
        nxt = [dot(lp[i], lp[i]) for i in n] if 2 * span < c else None
        u = [u[i] + sign * dot(lp[i], b16(u[i])) for i in n]
        if nxt is not None:
            lp = [b16(x) for x in nxt]
        sign = 1.0
        span *= 2
    ux = [jnp.concatenate([b16(u[i]), vx[i]], axis=0) for i in n]
    y = [qg[i][c2:2 * c2] + dot(mbk[i], ux[i]) for i in n]
    ds_ = [lax.dot_general(ux[i], nbk[i], _TN, preferred_element_type=F32) for i in n]
    return [(y[i][0:c] + y[i][c:c2], (items[i][0] + ds_[i]) * items[i][6]) for i in n]


def _pair_state(s0, s1):
    z = jnp.zeros((RWKV_HEAD, RWKV_HEAD), F32)
    return jnp.concatenate([jnp.concatenate([s0, z], axis=1), jnp.concatenate([z, s1], axis=1)], axis=0)


def _make_wkv_prompt_kernel(c, nchunk, g, t_real):
    rb = c * nchunk

    def kern(r_ref, lw_ref, k_ref, v_ref, kk_ref, b_ref, y_ref, so_ref, s_ref):
        i = pl.program_id(2)

        @pl.when(i == 0)
        def _():
            s_ref[...] = jnp.zeros(s_ref.shape, F32)

        masks = _wkv_masks(c)
        tril = (lax.broadcasted_iota(jnp.int32, (c, c), 0) >= lax.broadcasted_iota(jnp.int32, (c, c), 1)).astype(BF16)
        nreal = jnp.clip((t_real - i * rb + c - 1) // c, 0, nchunk)

        @pl.when(nreal < nchunk)
        def _():
            y_ref[...] = jnp.zeros(y_ref.shape, F32)

        def chunk(ci, carry):
            rows = pl.ds(pl.multiple_of(ci * c, c), c)
            lw = lw_ref[rows, :]
            hi, lo = _split(lw)
            cs = jnp.dot(tril, hi, preferred_element_type=F32) + jnp.dot(tril, lo, preferred_element_type=F32)
            e_pos = jnp.exp(cs)
            e_neg = jnp.exp(-cs)
            rt = r_ref[rows, :] * e_pos
            qh = kk_ref[rows, :] * jnp.exp(cs - lw)
            bt = b_ref[rows, :] * e_neg
            kt = k_ref[rows, :] * e_neg
            vv = v_ref[rows, :]
            items = []
            for p in range(g):
                sl = slice(p * PAIR, (p + 1) * PAIR)
                items.append((s_ref[p], rt[:, sl], qh[:, sl], bt[:, sl], kt[:, sl], vv[:, sl], e_pos[c - 1:c, sl]))
            outs = _wkv_pair_chunks(items, masks, c)
            for p in range(g):
                s_ref[p] = outs[p][1]
                y_ref[rows, p * PAIR:(p + 1) * PAIR] = outs[p][0]
            return carry

        lax.fori_loop(0, nreal, chunk, 0)

        @pl.when(i == pl.num_programs(2) - 1)
        def _():
            for p in range(g):
                s = s_ref[p]
                so_ref[0, 2 * p] = s[0:RWKV_HEAD, 0:RWKV_HEAD]
                so_ref[0, 2 * p + 1] = s[RWKV_HEAD:PAIR, RWKV_HEAD:PAIR]

    return kern


def _wkv_prompt(seqs, nb, tp, t_real, c=64, nchunk=4, g=8):
    d = seqs[0].shape[1]
    nh = d // RWKV_HEAD
    rb = c * nchunk
    nrb = tp // rb
    spec = pl.BlockSpec((rb, g * PAIR), lambda n, q, i: (n * nrb + i, q))
    return pl.pallas_call(
        _make_wkv_prompt_kernel(c, nchunk, g, t_real),
        grid=(nb, nh // (2 * g), nrb),
        in_specs=[spec] * 6,
        out_specs=[spec, pl.BlockSpec((1, 2 * g, RWKV_HEAD, RWKV_HEAD), lambda n, q, i: (n, q, 0, 0))],
        out_shape=[jax.ShapeDtypeStruct((nb * tp, d), F32),
                   jax.ShapeDtypeStruct((nb, nh, RWKV_HEAD, RWKV_HEAD), F32)],
        scratch_shapes=[pltpu.VMEM((g, PAIR, PAIR), F32)],
        compiler_params=_params(("parallel", "parallel", "arbitrary")),
        name="wkv_prompt",
    )(*seqs)


def _make_wkv_sample_kernel(c, ns, g):
    def kern(r_ref, lw_ref, k_ref, v_ref, kk_ref, b_ref, s0_ref, y_ref, so_ref):
        masks = _wkv_masks(c)
        row = lax.broadcasted_iota(jnp.int32, (c, g * PAIR), 0)
        items = []
        for n in range(ns):
            rows = slice(n * c, (n + 1) * c)
            lw = lw_ref[rows, :]
            cs = jnp.zeros(lw.shape, F32)
            for j in range(c):
                cs = cs + jnp.where(row >= j, lw[j:j + 1, :], 0.0)
            e_pos = jnp.exp(cs)
            e_neg = jnp.exp(-cs)
            rt = r_ref[rows, :] * e_pos
            qh = kk_ref[rows, :] * jnp.exp(cs - lw)
            bt = b_ref[rows, :] * e_neg
            kt = k_ref[rows, :] * e_neg
            vv = v_ref[rows, :]
            for p in range(g):
                sl = slice(p * PAIR, (p + 1) * PAIR)
                s = _pair_state(s0_ref[n, 2 * p], s0_ref[n, 2 * p + 1])
                items.append((s, rt[:, sl], qh[:, sl], bt[:, sl], kt[:, sl], vv[:, sl], e_pos[c - 1:c, sl]))
        outs = _wkv_pair_chunks(items, masks, c)
        for n in range(ns):
            for p in range(g):
                y, s = outs[n * g + p]
                y_ref[n * c:(n + 1) * c, p * PAIR:(p + 1) * PAIR] = y
                so_ref[n, 2 * p] = s[0:RWKV_HEAD, 0:RWKV_HEAD]
                so_ref[n, 2 * p + 1] = s[RWKV_HEAD:PAIR, RWKV_HEAD:PAIR]

    return kern


def _wkv_sample(seqs, s0, c, ns=4, g=4):
    d = seqs[0].shape[1]
    n = s0.shape[0]
    nh = d // RWKV_HEAD
    spec = pl.BlockSpec((ns * c, g * PAIR), lambda i, q: (i, q))
    st = pl.BlockSpec((ns, 2 * g, RWKV_HEAD, RWKV_HEAD), lambda i, q: (i, q, 0, 0))
    return pl.pallas_call(
        _make_wkv_sample_kernel(c, ns, g),
        grid=(n // ns, nh // (2 * g)),
        in_specs=[spec] * 6 + [st],
        out_specs=[spec, st],
        out_shape=[jax.ShapeDtypeStruct((n * c, d), F32),
                   jax.ShapeDtypeStruct((n, nh, RWKV_HEAD, RWKV_HEAD), F32)],
        compiler_params=_params(("parallel", "parallel")),
        name="wkv_sample",
    )(*seqs, s0)


def _rwkv_post_kernel(y_ref, r_ref, k_ref, v_ref, g_ref, lng_ref, lnb_ref, rk_ref, e_ref, et_ref, o_ref):
    e = e_ref[...]
    et = et_ref[...]
    inv = 1.0 / RWKV_HEAD
    y = y_ref[...]
    mu = _dot2(_dot2(y, e), et) * inv
    dlt = y - mu
    var = _dot2(_dot2(dlt * dlt, e), et) * inv
    yn = dlt * lax.rsqrt(var + RWKV_GN_EPS) * lng_ref[...] + lnb_ref[...]
    v = v_ref[...]
    bonus = _dot2(_dot2(r_ref[...] * k_ref[...] * rk_ref[...], e), et) * v
    o_ref[...] = ((yn + bonus) * g_ref[...]).astype(BF16)


def _rwkv_post(y, r, k2, v, g, prm, tr=128):
    nrows, d = y.shape
    e64, et64 = _seg_matrix(d, RWKV_HEAD)
    row = pl.BlockSpec((tr, d), lambda i: (i, 0))
    vec = pl.BlockSpec((1, d), lambda i: (0, 0))
    return pl.pallas_call(
        _rwkv_post_kernel,
        grid=(nrows // tr,),
        in_specs=[row] * 5 + [vec] * 3 + [pl.BlockSpec(e64.shape, lambda i: (0, 0)),
                                          pl.BlockSpec(et64.shape, lambda i: (0, 0))],
        out_specs=row,
        out_shape=jax.ShapeDtypeStruct((nrows, d), BF16),
        compiler_params=_params(("parallel",)),
        name="rwkv_post",
    )(y, r, k2, v, g, prm["ln_g"], prm["ln_b"], prm["r_k"], e64, et64)


def _make_mla_prep_kernel(prompt, nh):
    inv_dim = 1.0 / QK_DIM

    def kern(*refs):
        (qn_ref, q1_ref, q2_ref, c_ref, kp_ref, cos_ref, sin_ref, c128_ref, s128_ref,
         gqn_ref, gqr_ref, gkv_ref, e128_ref, et128_ref, e32_ref, et32_ref, p1_ref, p2_ref) = refs[:18]
        if prompt:
            (gkn_ref, gkr_ref, wuk_ref, wuv_ref, qo_ref, co_ref, ko_ref, kout_ref, vout_ref) = refs[18:]
        else:
            (qo_ref, co_ref, ko_ref) = refs[18:]
        scale = QK_DIM ** -0.5
        qn = qn_ref[...]
        cos = cos_ref[...]
        sin = sin_ref[...]
        q1 = q1_ref[...]
        q2 = q2_ref[...]
        r1 = q1 * cos - q2 * sin
        r2 = q2 * cos + q1 * sin
        ssq = _dot2(qn * qn, e128_ref[...]) + _dot2(r1 * r1, e32_ref[...]) + _dot2(r2 * r2, e32_ref[...])
        rf = lax.rsqrt(ssq * inv_dim + NORM_EPS) * scale
        qn = qn * _dot2(rf, et128_ref[...]) * gqn_ref[...]
        rf32 = _dot2(rf, et32_ref[...]) * gqr_ref[...]
        r1 = (r1 * rf32).astype(BF16)
        r2 = (r2 * rf32).astype(BF16)
        qr = (jnp.dot(r1, p1_ref[...], preferred_element_type=F32)
              + jnp.dot(r2, p2_ref[...], preferred_element_type=F32)).astype(BF16)
        qnb = qn.astype(BF16)
        for h in range(nh):
            qo_ref[:, h * QK_PAD:h * QK_PAD + QK_NOPE] = qnb[:, h * QK_NOPE:(h + 1) * QK_NOPE]
            qo_ref[:, h * QK_PAD + QK_NOPE:(h + 1) * QK_PAD] = qr[:, h * 128:(h + 1) * 128]
        c = c_ref[...]
        c = c * lax.rsqrt(jnp.mean(c * c, axis=-1, keepdims=True) + NORM_EPS) * gkv_ref[...]
        co_ref[...] = c
        kp = kp_ref[...]
        lane = lax.broadcasted_iota(jnp.int32, kp.shape, 1)
        swap = jnp.where(lane < QK_ROPE // 2, pltpu.roll(kp, 128 - QK_ROPE // 2, axis=1),
                         pltpu.roll(kp, QK_ROPE // 2, axis=1))
        kr = kp * c128_ref[...] + swap * s128_ref[...]
        ko_ref[...] = kr[:, :QK_ROPE]
        if prompt:
            cb = c.astype(BF16)
            kn = jnp.dot(cb, wuk_ref[...], preferred_element_type=F32)
            ssqk = _dot2(kn * kn, e128_ref[...]) + jnp.sum(kr * kr, axis=-1, keepdims=True)
            rk = _dot2(lax.rsqrt(ssqk * inv_dim + NORM_EPS), et128_ref[...])
            knb = (kn * rk * gkn_ref[...]).astype(BF16)
            krg = kr * gkr_ref[...]
            krb = (jnp.concatenate([krg] * nh, axis=1) * rk).astype(BF16)
            for h in range(nh):
                kout_ref[:, h * QK_PAD:h * QK_PAD + QK_NOPE] = knb[:, h * QK_NOPE:(h + 1) * QK_NOPE]
                kout_ref[:, h * QK_PAD + QK_NOPE:(h + 1) * QK_PAD] = krb[:, h * 128:(h + 1) * 128]
            vb = jnp.dot(cb, wuv_ref[...], preferred_element_type=F32).astype(BF16)
            ones = jnp.ones((vb.shape[0], V_HEAD), BF16)
            for h in range(nh):
                vout_ref[:, 2 * h * V_HEAD:(2 * h + 1) * V_HEAD] = vb[:, h * V_HEAD:(h + 1) * V_HEAD]
                vout_ref[:, (2 * h + 1) * V_HEAD:(2 * h + 2) * V_HEAD] = ones

    return kern


def _mla_prep(cols, lay, row0, nrows, prompt, tabs, prm, tr=256):
    nh = lay["mla_heads"]
    rb0 = row0 // tr
    cb = lay["cb"]
    dq = nh * QK_NOPE
    dr = nh * (QK_ROPE // 2)
    kvl = prm["kv_g"].shape[1]

    def colspec(width, blk):
        return pl.BlockSpec((tr, width), lambda i: (rb0 + i, blk))

    def tabspec(width):
        return pl.BlockSpec((tr, width), lambda i: (rb0 + i, 0))

    def full(a_):
        return pl.BlockSpec(a_.shape, lambda i: (0,) * a_.ndim)

    e128, et128 = _seg_matrix(dq, QK_NOPE)
    e32, et32 = _seg_matrix(dr, QK_ROPE // 2)
    idx = np.arange(dr)
    p1 = np.zeros((dr, nh * 128), np.float32)
    p2 = np.zeros((dr, nh * 128), np.float32)
    p1[idx, (idx // 32) * 128 + idx % 32] = 1.0
    p2[idx, (idx // 32) * 128 + 32 + idx % 32] = 1.0
    p1 = jnp.asarray(p1, BF16)
    p2 = jnp.asarray(p2, BF16)

    args = [cols, cols, cols, cols, cols, tabs["cos"], tabs["sin"], tabs["c128"], tabs["s128"],
            prm["gq_n"], prm["gq_r"], prm["kv_g"], e128, et128, e32, et32, p1, p2]
    in_specs = [colspec(dq, cb["q_nope"]), colspec(dr, cb["q_r1"]), colspec(dr, cb["q_r2"]),
                colspec(kvl, cb["c"]), colspec(128, cb["kpe"]),
                tabspec(dr), tabspec(dr), tabspec(128), tabspec(128)]
    in_specs += [full(a_) for a_ in args[9:]]
    row = lambda w_: pl.BlockSpec((tr, w_), lambda i: (i, 0))
    out_specs = [row(nh * QK_PAD), row(kvl), row(QK_ROPE)]
    out_shape = [jax.ShapeDtypeStruct((nrows, nh * QK_PAD), BF16),
                 jax.ShapeDtypeStruct((nrows, kvl), F32),
                 jax.ShapeDtypeStruct((nrows, QK_ROPE), F32)]
    if prompt:
        extra = [prm["gk_n"], prm["gk_r128"], prm["w_uk"], prm["w_uv"]]
        args += extra
        in_specs += [full(a_) for a_ in extra]
        out_specs += [row(nh * QK_PAD), row(nh * 2 * V_HEAD)]
        out_shape += [jax.ShapeDtypeStruct((nrows, nh * QK_PAD), BF16),
                      jax.ShapeDtypeStruct((nrows, nh * 2 * V_HEAD), BF16)]
    return pl.pallas_call(
        _make_mla_prep_kernel(prompt, nh),
        grid=(nrows // tr,),
        in_specs=in_specs,
        out_specs=out_specs,
        out_shape=out_shape,
        compiler_params=_params(("parallel",)),
        name="mla_prep_prompt" if prompt else "mla_prep_sample",
    )(*args)


def _make_prompt_attn_kernel(tile, hb):
    def kern(q_ref, k_ref, v_ref, o_ref, m_ref, acc_ref):
        qi = pl.program_id(2)
        m_ref[...] = jnp.full(m_ref.shape, NEG_INF, F32)
        acc_ref[...] = jnp.zeros(acc_ref.shape, F32)
        heads = range(hb)
        qs = [q_ref[:, h * QK_PAD:(h + 1) * QK_PAD] for h in heads]
        causal = (lax.broadcasted_iota(jnp.int32, (tile, tile), 1) <= lax.broadcasted_iota(jnp.int32, (tile, tile), 0))

        def kv_tile(j, diagonal):
            rows = pl.ds(pl.multiple_of(j * tile, tile), tile)
            ss = [lax.dot_general(qs[h], k_ref[rows, h * QK_PAD:(h + 1) * QK_PAD], _NT,
                                  preferred_element_type=F32) for h in heads]
            if diagonal:
                ss = [jnp.where(causal, s, NEG_INF) for s in ss]
            ms = [m_ref[h] for h in heads]
            mn = [jnp.maximum(ms[h], jnp.max(ss[h], axis=1, keepdims=True)) for h in heads]
            ps = [jnp.exp(ss[h] - jnp.concatenate([mn[h]] * (tile // 128), axis=1)) for h in heads]
            pv = [jnp.dot(ps[h].astype(BF16), v_ref[rows, h * 2 * V_HEAD:(h + 1) * 2 * V_HEAD],
                          preferred_element_type=F32) for h in heads]
            for h in heads:
                alpha = jnp.exp(ms[h] - mn[h])
                acc_ref[h] = jnp.concatenate([alpha, alpha], axis=1) * acc_ref[h] + pv[h]
                m_ref[h] = mn[h]

        def body(j, carry):
            kv_tile(j, False)
            return carry

        lax.fori_loop(0, qi, body, 0)
        kv_tile(qi, True)
        for h in heads:
            acc = acc_ref[h]
            o_ref[:, h * V_HEAD:(h + 1) * V_HEAD] = (acc[:, :V_HEAD] / acc[:, V_HEAD:]).astype(BF16)

    return kern


def _prompt_attn(q, k, v, nb, tp, nh, tile=ATTN_TILE, hb=4):
    nq = tp // tile
    return pl.pallas_call(
        _make_prompt_attn_kernel(tile, hb),
        grid=(nb, nh // hb, nq),
        in_specs=[pl.BlockSpec((tile, hb * QK_PAD), lambda b, h, i: (b * nq + i, h)),
                  pl.BlockSpec((tp, hb * QK_PAD), lambda b, h, i: (b, h)),
                  pl.BlockSpec((tp, hb * 2 * V_HEAD), lambda b, h, i: (b, h))],
        out_specs=pl.BlockSpec((tile, hb * V_HEAD), lambda b, h, i: (b * nq + i, h)),
        out_shape=jax.ShapeDtypeStruct((nb * tp, nh * V_HEAD), BF16),
        scratch_shapes=[pltpu.VMEM((hb, tile, 128), F32), pltpu.VMEM((hb, tile, 2 * V_HEAD), F32)],
        compiler_params=_params(("parallel", "parallel", "arbitrary")),
        name="prompt_attn",
    )(q, k, v)


def _sample_q_kernel(q_ref, gkn_ref, gkr_ref, wuk_ref, qa_ref, qr_ref):
    q = q_ref[...].astype(F32)
    qn = (q[:, :QK_NOPE] * gkn_ref[...]).astype(BF16)
    qa_ref[...] = lax.dot_general(qn, wuk_ref[...], (((1,), (1,)), ((), ())),
                                  preferred_element_type=F32).astype(BF16)
    qr_ref[...] = (q[:, QK_NOPE:] * gkr_ref[...]).astype(BF16)


def _sample_q(q, prm, nh):
    rs = q.shape[0]
    kvl = prm["w_uk"].shape[0]
    return pl.pallas_call(
        _sample_q_kernel,
        grid=(nh,),
        in_specs=[pl.BlockSpec((rs, QK_PAD), lambda h: (0, h)),
                  pl.BlockSpec((1, QK_NOPE), lambda h: (0, 0)),
                  pl.BlockSpec((1, 128), lambda h: (0, 0)),
                  pl.BlockSpec((kvl, QK_NOPE), lambda h: (0, h))],
        out_specs=[pl.BlockSpec((rs, kvl), lambda h: (0, h)),
                   pl.BlockSpec((rs, 128), lambda h: (0, h))],
        out_shape=[jax.ShapeDtypeStruct((rs, nh * kvl), BF16),
                   jax.ShapeDtypeStruct((rs, nh * 128), BF16)],
        compiler_params=_params(("parallel",)),
        name="sample_q",
    )(q, prm["gk_n1"], prm["gk_r128"], prm["w_uk"])


SA_SLOTS = 16
SA_UNROLL = 4


def _make_sample_attn_kernel(nh, ds, kvl, page, n_pages, chunk_pages):
    nq = nh * ds
    nw = nh * QK_NOPE
    ct = chunk_pages * page
    nch = n_pages // chunk_pages
    inv_dim = 1.0 / QK_DIM

    new = SA_SLOTS
    pre = SA_SLOTS - SA_UNROLL
    assert nch % SA_SLOTS == 0 and nch % SA_UNROLL == 0 and SA_UNROLL % 2 == 0 and pre > SA_UNROLL

    def kern(pt_ref, wt_ref, qa_ref, qr_ref, cnew_ref, rnew_ref, cc_ref, cr_ref, o_ref,
             l_ref, cbuf, rbuf, kt0, kt1, a20, a21, sem):
        n = pl.program_id(0)
        nseq = pl.num_programs(0)

        def copies(seq, j, slot):
            out = []
            for p in range(chunk_pages):
                pg = pt_ref[seq * n_pages + j * chunk_pages + p]
                out.append(pltpu.make_async_copy(cc_ref.at[pg], cbuf.at[slot, pl.ds(p * page, page)],
                                                 sem.at[slot, 2 * p]))
                out.append(pltpu.make_async_copy(cr_ref.at[pg], rbuf.at[slot, :, pl.ds(p * page, page)],
                                                 sem.at[slot, 2 * p + 1]))
            return out

        def request(t):
            seq2 = n + t // nch
            ch2 = t % nch

            @pl.when(seq2 < nseq)
            def _():
                for cp in copies(seq2, ch2, ch2 % SA_SLOTS):
                    cp.start()

        def arrive(j):
            @pl.when(j < nch)
            def _():
                for cp in copies(n, j, j % SA_SLOTS):
                    cp.wait()

        @pl.when(n == 0)
        def _():
            l_ref[0:nw, :] = wt_ref[...]
            for j in range(pre):
                for cp in copies(0, j, j):
                    cp.start()

        l_ref[nw:nw + nq, :] = qa_ref[0]
        cbuf[new] = jnp.zeros((ct, kvl), F32)
        rbuf[new] = jnp.zeros((QK_ROPE, ct), F32)
        cbuf[new, 0:8, :] = cnew_ref[0]
        rbuf[new, :, 0:128] = rnew_ref[0]

        def scores(j, kt_ref, a2_ref):
            slot = jnp.where(j < nch, j % SA_SLOTS, new)
            cb = cbuf[slot].astype(BF16)
            rb = rbuf[slot].astype(BF16)
            kt_ref[...] = lax.dot_general(l_ref[...], cb, _NT, preferred_element_type=F32)
            a2_ref[...] = jnp.dot(qr_ref[0], rb, preferred_element_type=F32)

        def softmax_pv(slot, kt_ref, a2_ref, carry, causal):
            m, l, acc = carry
            kn = kt_ref[0:nw, :].reshape(nh, QK_NOPE, ct)
            ssq = jnp.sum(kn * kn, axis=1)
            kpt = rbuf[slot]
            ssq = ssq + jnp.sum(kpt * kpt, axis=0, keepdims=True)
            rf = lax.rsqrt(ssq * inv_dim + NORM_EPS)
            s = (kt_ref[nw:nw + nq, :] + a2_ref[...]) * jnp.concatenate([rf] * ds, axis=0)
            if causal:
                qidx = lax.broadcasted_iota(jnp.int32, (nq, ct), 0) // nh
                kidx = lax.broadcasted_iota(jnp.int32, (nq, ct), 1)
                s = jnp.where(kidx <= qidx, s, NEG_INF)
            m_new = jnp.maximum(m, jnp.max(s, axis=1, keepdims=True))
            alpha = jnp.exp(m - m_new)
            p = jnp.exp(s - m_new)
            l = alpha * l + jnp.sum(p, axis=1, keepdims=True)
            acc = alpha * acc + jnp.dot(p.astype(BF16), cbuf[slot].astype(BF16), preferred_element_type=F32)
            return m_new, l, acc

        arrive(0)
        scores(0, kt0, a20)

        def body(i, carry):
            j = SA_UNROLL * i
            for k_ in range(SA_UNROLL):
                request(j + pre + k_)
            for k_ in range(1, SA_UNROLL + 1):
                arrive(j + k_)
            bufs = ((kt0, a20), (kt1, a21))
            for k_ in range(SA_UNROLL):
                scores(j + k_ + 1, *bufs[(k_ + 1) % 2])
                carry = softmax_pv((j + k_) % SA_SLOTS, *bufs[k_ % 2], carry, False)
            return carry

        init = (jnp.full((nq, 1), NEG_INF, F32), jnp.zeros((nq, 1), F32), jnp.zeros((nq, kvl), F32))
        carry = lax.fori_loop(0, nch // SA_UNROLL, body, init)
        m, l, acc = softmax_pv(new, kt0, a20, carry, True)
        o_ref[0] = acc / l

    return kern


def _sample_attn(page_table, wt, qa, qr, cnew, rnew_t, cache_c, cache_rt, nh, ds, chunk_pages=2):
    db, n_pages = page_table.shape
    _, page, kvl = cache_c.shape
    nq = nh * ds
    nw = nh * QK_NOPE
    ct = chunk_pages * page
    assert page == 128
    grid_spec = pltpu.PrefetchScalarGridSpec(
        num_scalar_prefetch=1,
        grid=(db,),
        in_specs=[pl.BlockSpec((nw, kvl), lambda n, pt: (0, 0)),
                  pl.BlockSpec((1, nq, kvl), lambda n, pt: (n, 0, 0)),
                  pl.BlockSpec((1, nq, QK_ROPE), lambda n, pt: (n, 0, 0)),
                  pl.BlockSpec((1, 8, kvl), lambda n, pt: (n, 0, 0)),
                  pl.BlockSpec((1, QK_ROPE, 128), lambda n, pt: (n, 0, 0)),
                  pl.BlockSpec(memory_space=pl.ANY),
                  pl.BlockSpec(memory_space=pl.ANY)],
        out_specs=pl.BlockSpec((1, nq, kvl), lambda n, pt: (n, 0, 0)),
        scratch_shapes=[pltpu.VMEM((nw + nq, kvl), BF16),
                        pltpu.VMEM((SA_SLOTS + 1, ct, kvl), F32),
                        pltpu.VMEM((SA_SLOTS + 1, QK_ROPE, ct), F32),
                        pltpu.VMEM((nw + nq, ct), F32), pltpu.VMEM((nw + nq, ct), F32),
                        pltpu.VMEM((nq, ct), F32), pltpu.VMEM((nq, ct), F32),
                        pltpu.SemaphoreType.DMA((SA_SLOTS, 2 * chunk_pages))],
    )
    return pl.pallas_call(
        _make_sample_attn_kernel(nh, ds, kvl, page, n_pages, chunk_pages),
        grid_spec=grid_spec,
        out_shape=jax.ShapeDtypeStruct((db, nq, kvl), F32),
        compiler_params=_params(("arbitrary",)),
        name="sample_attn",
    )(page_table.reshape(-1), wt, qa, qr, cnew, rnew_t, cache_c, cache_rt)


def _sample_o_kernel(ol_ref, wuv_ref, o_ref):
    o_ref[...] = jnp.dot(ol_ref[...].astype(BF16), wuv_ref[...], preferred_element_type=F32).astype(BF16)


def _sample_o(olat, w_uv, nh):
    rs = olat.shape[0]
    kvl = w_uv.shape[0]
    return pl.pallas_call(
        _sample_o_kernel,
        grid=(nh,),
        in_specs=[pl.BlockSpec((rs, kvl), lambda h: (0, h)),
                  pl.BlockSpec((kvl, V_HEAD), lambda h: (0, h))],
        out_specs=pl.BlockSpec((rs, V_HEAD), lambda h: (0, h)),
        out_shape=jax.ShapeDtypeStruct((rs, nh * V_HEAD), BF16),
        compiler_params=_params(("parallel",)),
        name="sample_o",
    )(olat, w_uv)


def _merge_kernel(ya_ref, yb_ref, ga_ref, gb_ref, wa_ref, wb_ref, o_ref):
    pa = jnp.dot(ya_ref[...], wa_ref[...], preferred_element_type=F32)
    pb = jnp.dot(yb_ref[...], wb_ref[...], preferred_element_type=F32)
    o_ref[...] = (_sigmoid(ga_ref[...]) * pa + _sigmoid(gb_ref[...]) * pb).astype(BF16)


def _merge(ya, yb, cols, lay, wa, wb, tm=512, tn=512):
    r, d = ya.shape
    nj = d // tn
    ga0 = lay["cb"]["gate_a"] * (d // tn)
    gb0 = lay["cb"]["gate_b"] * (d // tn)
    return pl.pallas_call(
        _merge_kernel,
        grid=(nj, r // tm),
        in_specs=[pl.BlockSpec((tm, d), lambda j, i: (i, 0)),
                  pl.BlockSpec((tm, d), lambda j, i: (i, 0)),
                  pl.BlockSpec((tm, tn), lambda j, i: (i, ga0 + j)),
                  pl.BlockSpec((tm, tn), lambda j, i: (i, gb0 + j)),
                  pl.BlockSpec((d, tn), lambda j, i: (0, j)),
                  pl.BlockSpec((d, tn), lambda j, i: (0, j))],
        out_specs=pl.BlockSpec((tm, tn), lambda j, i: (i, j)),
        out_shape=jax.ShapeDtypeStruct((r, d), BF16),
        compiler_params=_params(("parallel", "parallel")),
        name="branch_merge",
    )(ya, yb, cols, cols, wa, wb)


def _make_out_router_kernel(n_experts):
    def kern(m_ref, x_ref, wo_ref, g2_ref, wrh_ref, wrl_ref, br_ref, x1_ref, h_ref, gate_ref, idx_ref):
        x1 = x_ref[...] + jnp.dot(m_ref[...], wo_ref[...], preferred_element_type=F32)
        x1_ref[...] = x1
        h = x1 * lax.rsqrt(jnp.mean(x1 * x1, axis=-1, keepdims=True) + NORM_EPS) * g2_ref[...]
        h_ref[...] = h
        hh, hl = _split(h)
        logits = (jnp.dot(hh, wrh_ref[...], preferred_element_type=F32)
                  + jnp.dot(hh, wrl_ref[...], preferred_element_type=F32)
                  + jnp.dot(hl, wrh_ref[...], preferred_element_type=F32)) + br_ref[...]
        lane = lax.broadcasted_iota(jnp.int32, logits.shape, 1)
        work = jnp.where(lane < n_experts, logits, -jnp.inf)
        vals, idxs = [], []
        for _ in range(TOP_K):
            mx = jnp.max(work, axis=1, keepdims=True)
            ix = jnp.min(jnp.where(work == mx, lane, 128), axis=1, keepdims=True)
            vals.append(mx)
            idxs.append(ix)
            work = jnp.where(lane == ix, -jnp.inf, work)
        ex = [jnp.exp(v_ - vals[0]) for v_ in vals]
        den = ex[0] + ex[1] + ex[2] + ex[3]
        gate = jnp.zeros(logits.shape, F32)
        idx = jnp.zeros(logits.shape, jnp.int32)
        for k_ in range(TOP_K):
            gate = jnp.where(lane == k_, ex[k_] / den, gate)
            idx = jnp.where(lane == k_, idxs[k_], idx)
        gate_ref[...] = gate
        idx_ref[...] = idx

    return kern


def _out_router(merged, x, wo, ln2, wr_hi, wr_lo, br, n_experts, tm=256):
    r, d = x.shape
    row = pl.BlockSpec((tm, d), lambda i: (i, 0))
    small = pl.BlockSpec((tm, 128), lambda i: (i, 0))
    full = lambda a_: pl.BlockSpec(a_.shape, lambda i: (0, 0))
    return pl.pallas_call(
        _make_out_router_kernel(n_experts),
        grid=(r // tm,),
        in_specs=[row, row, full(wo), full(ln2), full(wr_hi), full(wr_lo), full(br)],
        out_specs=[row, row, small, small],
        out_shape=[jax.ShapeDtypeStruct((r, d), F32), jax.ShapeDtypeStruct((r, d), F32),
                   jax.ShapeDtypeStruct((r, 128), F32), jax.ShapeDtypeStruct((r, 128), jnp.int32)],
        compiler_params=_params(("parallel",)),
        name="out_proj_router",
    )(merged, x, wo, ln2, wr_hi, wr_lo, br)


def _make_gather_kernel(bm):
    def kern(nused_ref, tok_ref, nxt_ref, h_ref, o_ref, buf, sem):
        b = pl.program_id(0)
        slot = b % 2

        def fetch(idx_ref, s):
            def issue(i, carry):
                pltpu.make_async_copy(h_ref.at[pl.ds(idx_ref[0, 0, i], 1)], buf.at[s, pl.ds(i, 1)], sem.at[s]).start()
                return carry

            lax.fori_loop(0, bm, issue, 0, unroll=8)

        @pl.when((b == 0) & (nused_ref[0] > 0))
        def _():
            fetch(tok_ref, 0)

        @pl.when(b + 1 < nused_ref[0])
        def _():
            fetch(nxt_ref, 1 - slot)

        @pl.when(b < nused_ref[0])
        def _():
            pltpu.make_async_copy(h_ref.at[pl.ds(0, bm)], buf.at[slot], sem.at[slot]).wait()
            o_ref[...] = buf[slot].astype(BF16)

        @pl.when(b >= nused_ref[0])
        def _():
            o_ref[...] = jnp.zeros(o_ref.shape, BF16)

    return kern


def _moe_gather(h, tok_sorted, nused, bm):
    mp = tok_sorted.shape[0]
    d = h.shape[1]
    nb = mp // bm
    tok3 = tok_sorted.reshape(nb, 1, bm)
    grid_spec = pltpu.PrefetchScalarGridSpec(
        num_scalar_prefetch=1,
        grid=(nb,),
        in_specs=[pl.BlockSpec((1, 1, bm), lambda b, nu: (b, 0, 0), memory_space=pltpu.SMEM),
                  pl.BlockSpec((1, 1, bm), lambda b, nu: (jnp.minimum(b + 1, nb - 1), 0, 0),
                               memory_space=pltpu.SMEM),
                  pl.BlockSpec(memory_space=pl.ANY)],
        out_specs=pl.BlockSpec((bm, d), lambda b, nu: (b, 0)),
        scratch_shapes=[pltpu.VMEM((2, bm, d), F32), pltpu.SemaphoreType.DMA((2,))],
    )
    return pl.pallas_call(
        _make_gather_kernel(bm),
        grid_spec=grid_spec,
        out_shape=jax.ShapeDtypeStruct((mp, d), BF16),
        compiler_params=_params(("arbitrary",)),
        name="moe_gather",
    )(nused, tok3, tok3, h)


WK_E, WK_J, WK_B, WK_JO, WK_VALID, WK_FIRST = range(6)


def _moe_up_kernel(wk_ref, x_ref, wg_ref, wu_ref, bg_ref, bu_ref, o_ref, wgb, wub):
    w = pl.program_id(0)

    @pl.when(wk_ref[WK_VALID, w] == 1)
    def _():
        @pl.when(wk_ref[WK_FIRST, w] == 1)
        def _():
            wgb[...] = wg_ref[0].astype(BF16)
            wub[...] = wu_ref[0].astype(BF16)

        x = x_ref[...]
        hg = jnp.dot(x, wgb[...], preferred_element_type=F32) + bg_ref[0]
        hu = jnp.dot(x, wub[...], preferred_element_type=F32) + bu_ref[0]
        gt = jnp.minimum(hg, SWIGLU_LIMIT)
        up = jnp.clip(hu, -SWIGLU_LIMIT, SWIGLU_LIMIT)
        o_ref[...] = ((up + 1.0) * gt * _sigmoid(SWIGLU_ALPHA * gt)).astype(BF16)

    @pl.when(wk_ref[WK_VALID, w] == 0)
    def _():
        o_ref[...] = jnp.zeros(o_ref.shape, BF16)


def _moe_up(xs, w_gu, b_gu, work, bm, tf):
    mp, d = xs.shape
    ne, _, f2 = w_gu.shape
    dff = f2 // 2
    nj = dff // tf
    nwork = work.shape[1]
    grid_spec = pltpu.PrefetchScalarGridSpec(
        num_scalar_prefetch=1,
        grid=(nwork,),
        in_specs=[pl.BlockSpec((bm, d), lambda w, wk: (wk[WK_B, w], 0)),
                  pl.BlockSpec((1, d, tf), lambda w, wk: (wk[WK_E, w], 0, wk[WK_J, w])),
                  pl.BlockSpec((1, d, tf), lambda w, wk: (wk[WK_E, w], 0, nj + wk[WK_J, w])),
                  pl.BlockSpec((1, 1, tf), lambda w, wk: (wk[WK_E, w], 0, wk[WK_J, w])),
                  pl.BlockSpec((1, 1, tf), lambda w, wk: (wk[WK_E, w], 0, nj + wk[WK_J, w]))],
        out_specs=pl.BlockSpec((bm, tf), lambda w, wk: (wk[WK_B, w], wk[WK_JO, w])),
        scratch_shapes=[pltpu.VMEM((d, tf), BF16), pltpu.VMEM((d, tf), BF16)],
    )
    return pl.pallas_call(
        _moe_up_kernel,
        grid_spec=grid_spec,
        out_shape=jax.ShapeDtypeStruct((mp, dff), BF16),
        compiler_params=_params(("arbitrary",)),
        name="moe_up",
    )(work, xs, w_gu, w_gu, b_gu, b_gu)


def _moe_down_kernel(wk_ref, a_ref, wd_ref, bd_ref, o_ref, wdb):
    w = pl.program_id(0)

    @pl.when(wk_ref[WK_VALID, w] == 1)
    def _():
        @pl.when(wk_ref[WK_FIRST, w] == 1)
        def _():
            wdb[...] = wd_ref[0].astype(BF16)

        o_ref[...] = jnp.dot(a_ref[...], wdb[...], preferred_element_type=F32) + bd_ref[0]

    @pl.when(wk_ref[WK_VALID, w] == 0)
    def _():
        o_ref[...] = jnp.zeros(o_ref.shape, F32)


def _moe_down(act, w_down, b_down, work, bm, tn):
    mp, dff = act.shape
    ne, _, d = w_down.shape
    nwork = work.shape[1]
    grid_spec = pltpu.PrefetchScalarGridSpec(
        num_scalar_prefetch=1,
        grid=(nwork,),
        in_specs=[pl.BlockSpec((bm, dff), lambda w, wk: (wk[WK_B, w], 0)),
                  pl.BlockSpec((1, dff, tn), lambda w, wk: (wk[WK_E, w], 0, wk[WK_J, w])),
                  pl.BlockSpec((1, 1, tn), lambda w, wk: (wk[WK_E, w], 0, wk[WK_J, w]))],
        out_specs=pl.BlockSpec((bm, tn), lambda w, wk: (wk[WK_B, w], wk[WK_JO, w])),
        scratch_shapes=[pltpu.VMEM((dff, tn), BF16)],
    )
    return pl.pallas_call(
        _moe_down_kernel,
        grid_spec=grid_spec,
        out_shape=jax.ShapeDtypeStruct((mp, d), F32),
        compiler_params=_params(("arbitrary",)),
        name="moe_down",
    )(work, act, w_down, b_down)


def _make_combine_kernel(tm):
    def kern(pos_ref, nxt_ref, x_ref, g_ref, ys_ref, o_ref, buf, sem):
        i = pl.program_id(0)
        slot = i % 2

        def fetch(idx_ref, s):
            def issue(r, carry):
                for k_ in range(TOP_K):
                    pltpu.make_async_copy(ys_ref.at[pl.ds(idx_ref[0, 0, r * TOP_K + k_], 1)],
                                          buf.at[s, k_, pl.ds(r, 1)], sem.at[s]).start()
                return carry

            lax.fori_loop(0, tm, issue, 0, unroll=4)

        @pl.when(i == 0)
        def _():
            fetch(pos_ref, 0)

        @pl.when(i + 1 < pl.num_programs(0))
        def _():
            fetch(nxt_ref, 1 - slot)

        for k_ in range(TOP_K):
            pltpu.make_async_copy(ys_ref.at[pl.ds(0, tm)], buf.at[slot, k_], sem.at[slot]).wait()
        g = g_ref[...]
        o_ref[...] = x_ref[...] + ((buf[slot, 0] * g[:, 0:1] + buf[slot, 1] * g[:, 1:2])
                                   + (buf[slot, 2] * g[:, 2:3] + buf[slot, 3] * g[:, 3:4]))

    return kern


def _moe_combine(x1, ys, pos, gate, tm=128):
    r, d = x1.shape
    nt = r // tm
    pos3 = pos.reshape(nt, 1, tm * TOP_K)
    return pl.pallas_call(
        _make_combine_kernel(tm),
        grid=(nt,),
        in_specs=[pl.BlockSpec((1, 1, tm * TOP_K), lambda i: (i, 0, 0), memory_space=pltpu.SMEM),
                  pl.BlockSpec((1, 1, tm * TOP_K), lambda i: (jnp.minimum(i + 1, nt - 1), 0, 0),
                               memory_space=pltpu.SMEM),
                  pl.BlockSpec((tm, d), lambda i: (i, 0)),
                  pl.BlockSpec((tm, 128), lambda i: (i, 0)),
                  pl.BlockSpec(memory_space=pl.ANY)],
        out_specs=pl.BlockSpec((tm, d), lambda i: (i, 0)),
        out_shape=jax.ShapeDtypeStruct((r, d), F32),
        scratch_shapes=[pltpu.VMEM((2, TOP_K, tm, d), F32), pltpu.SemaphoreType.DMA((2,))],
        compiler_params=_params(("arbitrary",)),
        name="moe_combine",
    )(pos3, pos3, x1, gate, ys)


def _moe_routing(idx, segments, n_rows, n_experts, bm, nj):
    n_real = sum(n_ for _, n_ in segments)
    m = n_real * TOP_K
    nb_total = -(-m // bm) + n_experts
    mp = nb_total * bm
    e_flat = jnp.concatenate([idx[s_:s_ + n_] for s_, n_ in segments], axis=0).reshape(m)
    tok_flat = jnp.asarray(np.repeat(np.concatenate([np.arange(s_, s_ + n_) for s_, n_ in segments]), TOP_K),
                           jnp.int32)
    order = jnp.argsort(e_flat)
    rank_all = jnp.argsort(order)
    ex = jnp.arange(n_experts, dtype=jnp.int32)
    counts = jnp.sum((e_flat[:, None] == ex[None, :]).astype(jnp.int32), axis=0)
    starts = jnp.cumsum(counts) - counts
    nblk = (counts + bm - 1) // bm
    blk_ends = jnp.cumsum(nblk)
    blk_starts = blk_ends - nblk
    nused = blk_ends[-1]
    slot = (blk_starts[e_flat] * bm + (rank_all - starts[e_flat])).astype(jnp.int32).reshape(n_real, TOP_K)
    pieces, at, off = [], 0, 0
    for s_, n_ in segments:
        pieces += [jnp.zeros((s_ - at, TOP_K), jnp.int32), slot[off:off + n_]]
        at, off = s_ + n_, off + n_
    pos = jnp.concatenate(pieces + [jnp.zeros((n_rows - at, TOP_K), jnp.int32)], axis=0)
    blk_e = jnp.minimum(jnp.sum((blk_ends[None, :] <= jnp.arange(nb_total, dtype=jnp.int32)[:, None])
                                .astype(jnp.int32), axis=1), n_experts - 1)
    srow = jnp.arange(mp, dtype=jnp.int32)
    se = blk_e[srow // bm]
    srank = srow - blk_starts[se] * bm
    src = order[jnp.clip(starts[se] + srank, 0, m - 1)]
    tok_sorted = jnp.where(srank < counts[se], tok_flat[src], 0).astype(jnp.int32)
    w = jnp.arange(nb_total * nj, dtype=jnp.int32)
    valid = w < nused * nj
    wc = jnp.minimum(w, jnp.maximum(nused * nj - 1, 0))
    we = blk_e[wc // nj]
    local = wc - nj * blk_starts[we]
    nbe = jnp.maximum(nblk[we], 1)
    wj = local // nbe
    wf = ((local % nbe) == 0) & valid
    wb = jnp.where(valid, blk_starts[we] + local % nbe, w // nj)
    wjo = jnp.where(valid, wj, w % nj)
    work = jnp.stack([we, wj, wb, wjo, valid.astype(jnp.int32), wf.astype(jnp.int32)]).astype(jnp.int32)
    return tok_sorted, pos, nused.reshape(1).astype(jnp.int32), work


def _layout(d, nh_mla, kvl, n_w, n_a, n_g):
    assert d == 2048 and nh_mla * QK_NOPE == d and kvl == 512 and n_g == 256 and n_w <= 128 and n_a <= 128
    cb = {"q_nope": 0, "q_r1": 4, "q_r2": 5, "c": 6, "kpe": 28, "gate_a": 2, "gate_b": 3,
          "r": 4, "k": 5, "v": 6, "lora": 28}
    return {"d": d, "mla_heads": nh_mla, "cb": cb, "nc": 14848}


def _relayout_w_in(w, d, nh, kvl, n_w, n_a, n_g):
    o_w = 3 * d
    o_a = o_w + n_w
    o_g = o_a + n_a
    c1 = o_g + n_g
    c2 = c1 + nh * QK_DIM
    c3 = c2 + kvl
    c4 = c3 + QK_ROPE
    q = w[:, c1:c2].reshape(d, nh, QK_DIM)
    z = lambda n_: jnp.zeros((d, n_), w.dtype)
    half = QK_ROPE // 2
    parts = [q[:, :, :QK_NOPE].reshape(d, nh * QK_NOPE),
             q[:, :, QK_NOPE:QK_NOPE + half].reshape(d, nh * half),
             q[:, :, QK_NOPE + half:].reshape(d, nh * half),
             w[:, c2:c3], w[:, c3:c4], z(128 - QK_ROPE), z(384),
             w[:, c4:c4 + d], w[:, c4 + d:c4 + 2 * d],
             w[:, 0:3 * d],
             w[:, o_w:o_a], z(128 - n_w), w[:, o_a:o_g], z(128 - n_a), w[:, o_g:c1]]
    return jnp.concatenate(parts, axis=1).astype(BF16)


def _pad_lora_vec(vec, d, n_w, n_a, n_g):
    o_w = 3 * d
    o_a = o_w + n_w
    o_g = o_a + n_a
    z = lambda n_: jnp.zeros(vec.shape[:-1] + (n_,), vec.dtype)
    lora = jnp.concatenate([vec[..., o_w:o_a], z(128 - n_w), vec[..., o_a:o_g], z(128 - n_a),
                            vec[..., o_g:o_g + n_g]], axis=-1)
    return vec[..., 0:d], vec[..., d:2 * d], vec[..., 2 * d:3 * d], lora


def _unpad_shift(cols_row, lay, d, n_w, n_a, n_g):
    rkv = cols_row[..., 4 * d:7 * d]
    lo = cols_row[..., 7 * d:7 * d + 512]
    return jnp.concatenate([rkv, lo[..., 0:n_w], lo[..., 128:128 + n_a], lo[..., 256:256 + n_g]], axis=-1)


def _rope_tables(pos, nh):
    half = QK_ROPE // 2
    inv = ROPE_THETA ** (-jnp.arange(half, dtype=F32) / half)
    ang = pos.astype(F32)[:, None] * inv[None, :]
    cos, sin = jnp.cos(ang), jnp.sin(ang)
    z = jnp.zeros((pos.shape[0], 128 - QK_ROPE), F32)
    return {"cos": jnp.tile(cos, (1, nh)), "sin": jnp.tile(sin, (1, nh)),
            "c128": jnp.concatenate([cos, cos, z], axis=1),
            "s128": jnp.concatenate([-sin, sin, z], axis=1)}


def kernel(x_prompt, x_sample, cache_kv_latent, cache_k_rope, state_wkv, state_shift, page_table, meta_tokens,
           ln1_g, w_in, rw_mu, rw_w0, rw_w2, rw_a0, rw_a2, rw_g2, rw_k_k, rw_k_a, rw_r_k, rw_ln_g, rw_ln_b,
           q_norm_g, k_norm_g, kv_norm_g, w_kv_up, w_branch_a, w_branch_b, w_out, ln2_g,
           w_router, b_router, w_gu, b_gu, w_down, b_down):
    depth = w_in.shape[0]
    nb, seq, d = x_prompt.shape
    db, ds, _ = x_sample.shape
    n_meta = meta_tokens.shape[0]
    t = seq + n_meta
    tp = -(-t // ATTN_TILE) * ATTN_TILE
    rp = nb * tp
    rs = db * ds
    r_all = rp + rs
    n_pages = page_table.shape[1]
    page = cache_kv_latent.shape[2]
    kvl = cache_kv_latent.shape[3]
    past = n_pages * page
    nh_rw = d // RWKV_HEAD
    n_w, n_a, n_g = rw_w2.shape[1], rw_a2.shape[1], rw_g2.shape[1]
    nh = w_kv_up.shape[2] // (QK_NOPE + V_HEAD)
    n_experts = w_router.shape[2]
    lay = _layout(d, nh, kvl, n_w, n_a, n_g)
    assert r_all % 1024 == 0 and rs % 256 == 0 and ds <= 8

    meta = meta_tokens.astype(x_prompt.dtype)
    zpad = jnp.zeros((tp - t, d), x_prompt.dtype)
    x = jnp.concatenate([a_ for b_ in range(nb) for a_ in (meta, x_prompt[b_], zpad)] + [x_sample.reshape(rs, d)],
                        axis=0)

    pos_all = jnp.concatenate([jnp.tile(jnp.arange(tp), nb), jnp.tile(past + jnp.arange(ds), db)])
    tabs = _rope_tables(pos_all, nh)
    segments = [(b_ * tp, t) for b_ in range(nb)] + [(rp, rs)]

    outs = {k_: [] for k_ in ("lat_p", "kpe_p", "wkv_p", "sh_p", "lat_s", "kpe_s", "wkv_s", "sh_s")}
    for l in range(depth):
        w_in_l = _relayout_w_in(w_in[l], d, nh, kvl, n_w, n_a, n_g)
        cols = _in_proj(x, ln1_g[l][None], w_in_l)

        mu_r, mu_k, mu_v, mu_l = _pad_lora_vec(rw_mu[l][None], d, n_w, n_a, n_g)
        zrow = lambda a_, n_: jnp.concatenate([a_, jnp.zeros((128 - n_, d), a_.dtype)], axis=0)
        rprm = {"mu_r": mu_r, "mu_k": mu_k, "mu_v": mu_v, "mu_l": mu_l,
                "w0": rw_w0[l][None], "w2": zrow(rw_w2[l], n_w), "a0": rw_a0[l][None], "a2": zrow(rw_a2[l], n_a),
                "g2": rw_g2[l], "k_k": rw_k_k[l][None], "k_a": rw_k_a[l][None],
                "ln_g": rw_ln_g[l][None], "ln_b": rw_ln_b[l][None], "r_k": rw_r_k[l].reshape(1, d)}
        pp = _rwkv_prep(cols, lay, 0, rp, False, None, rprm, tp // 128, t, ds)
        sh = jnp.zeros((db, ds, state_shift.shape[2]), F32).at[:, 0].set(state_shift[l]).reshape(rs, -1)
        sp = _rwkv_prep(cols, lay, rp, rs, True, _pad_lora_vec(sh, d, n_w, n_a, n_g), rprm, 1, ds, ds)
        r_p, lw_p, k_p, v_p, kk_p, b_p, g_p = pp
        r_s, lw_s, k_s, v_s, kk_s, b_s, g_s = sp
        y_p, wkv_p = _wkv_prompt([r_p, lw_p, k_p, v_p, kk_p, b_p], nb, tp, t)
        seq_s = [jnp.pad(a_.reshape(db, ds, d), ((0, 0), (0, 8 - ds), (0, 0))).reshape(db * 8, d)
                 for a_ in (r_s, lw_s, k_s, v_s, kk_s, b_s)]
        y_s, wkv_s = _wkv_sample(seq_s, state_wkv[l], 8)
        y_s = y_s.reshape(db, 8, d)[:, :ds].reshape(rs, d)
        ya = jnp.concatenate([_rwkv_post(y_p, r_p, k_p, v_p, g_p, rprm),
                              _rwkv_post(y_s, r_s, k_s, v_s, g_s, rprm)], axis=0)

        w_up = w_kv_up[l].reshape(kvl, nh, QK_NOPE + V_HEAD)
        gq, gk = q_norm_g[l], k_norm_g[l]
        half = QK_ROPE // 2
        z64 = jnp.zeros((128 - QK_ROPE,), F32)
        mprm = {"gq_n": jnp.tile(gq[:QK_NOPE], nh)[None], "gq_r": jnp.tile(gq[QK_NOPE:], nh)[None],
                "gk_n": jnp.tile(gk[:QK_NOPE], nh)[None], "gk_n1": gk[:QK_NOPE][None],
                "gk_r128": jnp.concatenate([gk[QK_NOPE:], gk[QK_NOPE:], z64])[None],
                "kv_g": kv_norm_g[l][None],
                "w_uk": w_up[:, :, :QK_NOPE].reshape(kvl, nh * QK_NOPE).astype(BF16),
                "w_uv": w_up[:, :, QK_NOPE:].reshape(kvl, nh * V_HEAD).astype(BF16)}
        q_p, c_p, kpe_p, kx_p, vx_p = _mla_prep(cols, lay, 0, rp, True, tabs, mprm)
        q_s, c_s, kpe_s = _mla_prep(cols, lay, rp, rs, False, tabs, mprm)
        o_p = _prompt_attn(q_p, kx_p, vx_p, nb, tp, nh)

        qa, qr = _sample_q(q_s, mprm, nh)
        qa = qa.reshape(db, ds * nh, kvl)
        qr = qr.reshape(db, ds, nh, 128)[..., :QK_ROPE].reshape(db, ds * nh, QK_ROPE)
        pad8 = lambda a_: jnp.concatenate([a_, jnp.zeros((db, 8 - ds, a_.shape[-1]), F32)], axis=1)
        rnew_t = jnp.pad(jnp.swapaxes(kpe_s.reshape(db, ds, QK_ROPE), 1, 2), ((0, 0), (0, 0), (0, 128 - ds)))
        olat = _sample_attn(page_table, mprm["w_uk"].T, qa, qr, pad8(c_s.reshape(db, ds, kvl)), rnew_t,
                            cache_kv_latent[l], jnp.swapaxes(cache_k_rope[l], 1, 2), nh, ds)
        o_s = _sample_o(olat.reshape(rs, nh * kvl), mprm["w_uv"], nh)
        yb = jnp.concatenate([o_p, o_s], axis=0)

        merged = _merge(ya, yb, cols, lay, w_branch_a[l].astype(BF16), w_branch_b[l].astype(BF16))
        wr = jnp.concatenate([w_router[l], jnp.zeros((d, 128 - n_experts), F32)], axis=1)
        wr_hi = wr.astype(BF16)
        wr_lo = (wr - wr_hi.astype(F32)).astype(BF16)
        br = jnp.concatenate([b_router[l], jnp.zeros((128 - n_experts,), F32)])[None]
        x1, hmoe, gate, idx = _out_router(merged, x, w_out[l].astype(BF16), ln2_g[l][None], wr_hi, wr_lo, br, n_experts)

        dff = w_down.shape[2]
        assert d // MOE_TN == dff // MOE_TF
        tok_sorted, pos, nused, work = _moe_routing(idx[:, :TOP_K], segments, r_all, n_experts, MOE_BM,
                                                    dff // MOE_TF)
        xs = _moe_gather(hmoe, tok_sorted, nused, MOE_BM)
        act = _moe_up(xs, w_gu[l], b_gu[l][:, None, :], work, MOE_BM, MOE_TF)
        ys = _moe_down(act, w_down[l], b_down[l][:, None, :], work, MOE_BM, MOE_TN)
        x = _moe_combine(x1, ys, pos, gate)

        outs["lat_p"].append(c_p.reshape(nb, tp, kvl)[:, :t])
        outs["kpe_p"].append(kpe_p.reshape(nb, tp, QK_ROPE)[:, :t])
        outs["wkv_p"].append(wkv_p)
        last_p = np.arange(nb, dtype=np.int32) * tp + (t - 1)
        last_s = rp + np.arange(db, dtype=np.int32) * ds + (ds - 1)
        outs["sh_p"].append(_unpad_shift(cols[last_p], lay, d, n_w, n_a, n_g))
        outs["lat_s"].append(c_s.reshape(db, ds, kvl))
        outs["kpe_s"].append(kpe_s.reshape(db, ds, QK_ROPE))
        outs["wkv_s"].append(wkv_s)
        outs["sh_s"].append(_unpad_shift(cols[last_s], lay, d, n_w, n_a, n_g))

    y_prompt = jnp.stack([x[b_ * tp + n_meta:b_ * tp + t] for b_ in range(nb)])
    y_sample = x[rp:].reshape(db, ds, d)
    st = lambda k_: jnp.stack(outs[k_])
    return (y_prompt, y_sample, st("lat_p"), st("kpe_p"), st("wkv_p"), st("sh_p"),
            st("lat_s"), st("kpe_s"), st("wkv_s"), st("sh_s"))
```

```python
import functools

import numpy as np
import jax
import jax.numpy as jnp
from jax import lax
from jax.experimental import pallas as pl
from jax.experimental.pallas import tpu as pltpu

F32 = jnp.float32
BF16 = jnp.bfloat16

RWKV_HEAD = 64
QK_NOPE = 128
QK_ROPE = 64
QK_DIM = QK_NOPE + QK_ROPE
V_HEAD = 128
QK_PAD = 256
ROPE_THETA = 10000.0
RWKV_GN_EPS = 64e-5
NORM_EPS = 1e-6
NEG_INF = -1e30
TOP_K = 4
SWIGLU_LIMIT = 7.0
SWIGLU_ALPHA = 1.702

VMEM_LIMIT = 56 * 1024 * 1024

ATTN_TILE = 256
MOE_BM = 256
MOE_TF = 1024
MOE_TN = 1024


def _params(sem):
    return pltpu.CompilerParams(dimension_semantics=sem, vmem_limit_bytes=VMEM_LIMIT)


def _split(x):
    hi = x.astype(BF16)
    lo = (x - hi.astype(F32)).astype(BF16)
    return hi, lo


def _dot2(x, m):
    hi, lo = _split(x)
    return (jnp.dot(hi, m, preferred_element_type=F32) + jnp.dot(lo, m, preferred_element_type=F32))


def _sigmoid(x):
    return 1.0 / (1.0 + jnp.exp(-x))


def _seg_matrix(n, seg):
    e = (np.arange(n)[:, None] // seg == np.arange(n // seg)[None, :]).astype(np.float32)
    return jnp.asarray(e, BF16), jnp.asarray(e.T, BF16)


def _in_proj_kernel(x_ref, g_ref, w_ref, o_ref, xn_ref):
    @pl.when(pl.program_id(1) == 0)
    def _():
        x = x_ref[...]
        ms = jnp.mean(x * x, axis=-1, keepdims=True)
        xn_ref[...] = (x * lax.rsqrt(ms + NORM_EPS) * g_ref[...]).astype(BF16)

    o_ref[...] = jnp.dot(xn_ref[...], w_ref[...], preferred_element_type=F32)


def _in_proj(x, g, w, tm=1024, tn=512):
    r, d = x.shape
    nc = w.shape[1]
    return pl.pallas_call(
        _in_proj_kernel,
        grid=(r // tm, nc // tn),
        in_specs=[pl.BlockSpec((tm, d), lambda i, j: (i, 0)),
                  pl.BlockSpec((1, d), lambda i, j: (0, 0)),
                  pl.BlockSpec((d, tn), lambda i, j: (0, j))],
        out_specs=pl.BlockSpec((tm, tn), lambda i, j: (i, j)),
        out_shape=jax.ShapeDtypeStruct((r, nc), F32),
        scratch_shapes=[pltpu.VMEM((tm, d), BF16)],
        compiler_params=_params(("parallel", "arbitrary")),
        name="in_proj",
    )(x, g, w)


def _make_rwkv_prep_kernel(sample, tr, tiles_per_seq, t_real, ds):
    def kern(r_ref, k_ref, v_ref, l_ref, xr_ref, xk_ref, xv_ref, xl_ref,
             mur_ref, muk_ref, muv_ref, mul_ref, w0_ref, w2_ref, a0_ref, a2_ref, g2_ref,
             kk_ref, ka_ref, e_ref, et_ref,
             ro_ref, lwo_ref, ko_ref, vo_ref, kko_ref, bo_ref, go_ref):
        i = pl.program_id(0)

        def mixed(x_ref, extra_ref, mu_ref):
            x = x_ref[...]
            rolled = pltpu.roll(x, 1, axis=0)
            row = lax.broadcasted_iota(jnp.int32, x.shape, 0)
            if sample:
                prev = jnp.where(row % ds == 0, extra_ref[...], rolled)
            else:
                halo = jnp.where((i % tiles_per_seq) == 0, 0.0, extra_ref[7:8, :])
                prev = jnp.where(row == 0, halo, rolled)
            return x + mu_ref[...] * (prev - x)

        r = mixed(r_ref, xr_ref, mur_ref)
        k = mixed(k_ref, xk_ref, muk_ref)
        v = mixed(v_ref, xv_ref, muv_ref)
        lo = mixed(l_ref, xl_ref, mul_ref)
        w_in = lo[:, 0:128]
        a_in = lo[:, 128:256]
        g_in = lo[:, 256:512]
        z = w0_ref[...] + jnp.dot(jnp.tanh(w_in), w2_ref[...], preferred_element_type=F32)
        nz = -z
        softplus = jnp.maximum(nz, 0.0) + jnp.log(1.0 + jnp.exp(-jnp.abs(nz)))
        lw = -jnp.exp(-softplus - 0.5)
        a = _sigmoid(a0_ref[...] + jnp.dot(a_in, a2_ref[...], preferred_element_type=F32))
        g = jnp.dot(_sigmoid(g_in), g2_ref[...], preferred_element_type=F32)
        kk = k * kk_ref[...]
        ssq = _dot2(_dot2(kk * kk, e_ref[...]), et_ref[...])
        kk = kk * lax.rsqrt(jnp.maximum(ssq, 1e-24))
        k2 = k * (1.0 + (a - 1.0) * ka_ref[...])
        b = kk * a
        if not sample:
            row = lax.broadcasted_iota(jnp.int32, (tr, 1), 0) + (i % tiles_per_seq) * tr
            valid = row < t_real
            lw = jnp.where(valid, lw, 0.0)
            k2 = jnp.where(valid, k2, 0.0)
            kk = jnp.where(valid, kk, 0.0)
            b = jnp.where(valid, b, 0.0)
        ro_ref[...] = r
        lwo_ref[...] = lw
        ko_ref[...] = k2
        vo_ref[...] = v
        kko_ref[...] = kk
        bo_ref[...] = b
        go_ref[...] = g

    return kern


def _rwkv_prep(cols, lay, row0, nrows, sample, extras, prm, tiles_per_seq, t_real, ds, tr=128):
    d = lay["d"]
    rb0 = row0 // tr
    cb = lay["cb"]
    e64, et64 = _seg_matrix(d, RWKV_HEAD)

    def colspec(width, blk):
        return pl.BlockSpec((tr, width), lambda i: (rb0 + i, blk))

    in_specs = [colspec(d, cb["r"]), colspec(d, cb["k"]), colspec(d, cb["v"]), colspec(512, cb["lora"])]
    args = [cols, cols, cols, cols]
    if sample:
        for a_, w_ in zip(extras, (d, d, d, 512)):
            in_specs.append(pl.BlockSpec((tr, w_), lambda i: (i, 0)))
            args.append(a_)
    else:
        def halospec(width, blk):
            return pl.BlockSpec((8, width), lambda i: (jnp.maximum((rb0 + i) * (tr // 8) - 1, 0), blk))
        in_specs += [halospec(d, cb["r"]), halospec(d, cb["k"]), halospec(d, cb["v"]), halospec(512, cb["lora"])]
        args += [cols, cols, cols, cols]
    for name in ("mu_r", "mu_k", "mu_v", "mu_l", "w0", "w2", "a0", "a2", "g2", "k_k", "k_a"):
        a_ = prm[name]
        in_specs.append(pl.BlockSpec(a_.shape, lambda i: (0, 0)))
        args.append(a_)
    in_specs += [pl.BlockSpec(e64.shape, lambda i: (0, 0)), pl.BlockSpec(et64.shape, lambda i: (0, 0))]
    args += [e64, et64]
    out = jax.ShapeDtypeStruct((nrows, d), F32)
    return pl.pallas_call(
        _make_rwkv_prep_kernel(sample, tr, tiles_per_seq, t_real, ds),
        grid=(nrows // tr,),
        in_specs=in_specs,
        out_specs=[pl.BlockSpec((tr, d), lambda i: (i, 0))] * 7,
        out_shape=[out] * 7,
        compiler_params=_params(("parallel",)),
        name="rwkv_prep_sample" if sample else "rwkv_prep_prompt",
    )(*args)


_NT = (((1,), (1,)), ((), ()))
_TN = (((0,), (0,)), ((), ()))
PAIR = 2 * RWKV_HEAD


def _wkv_masks(c):
    rt = lax.broadcasted_iota(jnp.int32, (2 * c, 2 * c), 0)
    ct = lax.broadcasted_iota(jnp.int32, (2 * c, 2 * c), 1)
    same = (rt // c) == (ct // c)
    strict = same & ((ct % c) < (rt % c))
    incl = same & ((ct % c) <= (rt % c))
    lane = lax.broadcasted_iota(jnp.int32, (c, PAIR), 1)
    return strict, incl, lane < RWKV_HEAD


def _wkv_pair_chunks(items, masks, c):
    strict, incl, m0 = masks
    n = range(len(items))
    c2 = 2 * c
    dot = functools.partial(jnp.dot, preferred_element_type=F32)
    b16 = lambda x: x.astype(BF16)

    def ext(x):
        return jnp.concatenate([jnp.where(m0, x, 0.0), jnp.where(m0, 0.0, x)], axis=0).astype(BF16)

    qr = [jnp.concatenate([ext(it[2]), ext(it[1])], axis=0) for it in items]
    nbk = [jnp.concatenate([ext(-it[3]), ext(it[4])], axis=0) for it in items]
    vx = [ext(it[5]) for it in items]
    a = [lax.dot_general(qr[i], nbk[i], _NT, preferred_element_type=F32) for i in n]
    qg = [lax.dot_general(qr[i], b16(items[i][0]), _NT, preferred_element_type=F32) for i in n]
    lb = [jnp.where(strict, -a[i][0:c2, 0:c2], 0.0) for i in n]
    lk = [b16(jnp.where(strict, a[i][0:c2, c2:2 * c2], 0.0)) for i in n]
    mbk = [b16(jnp.concatenate([jnp.where(incl, a[i][c2:2 * c2, 0:c2], 0.0),
                                jnp.where(incl, a[i][c2:2 * c2, c2:2 * c2], 0.0)], axis=1)) for i in n]
    lp = [b16(x) for x in lb]
    u = [qg[i][0:c2] + dot(lk[i], vx[i]) for i in n]
    sign = -1.0
    span = 1
    while span < c:
        nxt = [dot(lp[i], lp[i]) for i in n] if 2 * span < c else None
        u = [u[i] + sign * dot(lp[i], b16(u[i])) for i in n]
        if nxt is not None:
            lp = [b16(x) for x in nxt]
        sign = 1.0
        span *= 2
    ux = [jnp.concatenate([b16(u[i]), vx[i]], axis=0) for i in n]
    y = [qg[i][c2:2 * c2] + dot(mbk[i], ux[i]) for i in n]
    ds_ = [lax.dot_general(ux[i], nbk[i], _TN, preferred_element_type=F32) for i in n]
    return [(y[i][0:c] + y[i][c:c2], (items[i][0] + ds_[i]) * items[i][6]) for i in n]


def _pair_state(s0, s1):
    z = jnp.zeros((RWKV_HEAD, RWKV_HEAD), F32)
    return jnp.concatenate([jnp.concatenate([s0, z], axis=1), jnp.concatenate([z, s1], axis=1)], axis=0)


def _make_wkv_prompt_kernel(c, nchunk, g, t_real):
    rb = c * nchunk

    def kern(r_ref, lw_ref, k_ref, v_ref, kk_ref, b_ref, y_ref, so_ref, s_ref):
        i = pl.program_id(2)

        @pl.when(i == 0)
        def _():
            s_ref[...] = jnp.zeros(s_ref.shape, F32)

        masks = _wkv_masks(c)
        tril = (lax.broadcasted_iota(jnp.int32, (c, c), 0) >= lax.broadcasted_iota(jnp.int32, (c, c), 1)).astype(BF16)
        nreal = jnp.clip((t_real - i * rb + c - 1) // c, 0, nchunk)

        @pl.when(nreal < nchunk)
        def _():
            y_ref[...] = jnp.zeros(y_ref.shape, F32)

        def chunk(ci, carry):
            rows = pl.ds(pl.multiple_of(ci * c, c), c)
            lw = lw_ref[rows, :]
            hi, lo = _split(lw)
            cs = jnp.dot(tril, hi, preferred_element_type=F32) + jnp.dot(tril, lo, preferred_element_type=F32)
            e_pos = jnp.exp(cs)
            e_neg = jnp.exp(-cs)
            rt = r_ref[rows, :] * e_pos
            qh = kk_ref[rows, :] * jnp.exp(cs - lw)
            bt = b_ref[rows, :] * e_neg
            kt = k_ref[rows, :] * e_neg
            vv = v_ref[rows, :]
            items = []
            for p in range(g):
                sl = slice(p * PAIR, (p + 1) * PAIR)
                items.append((s_ref[p], rt[:, sl], qh[:, sl], bt[:, sl], kt[:, sl], vv[:, sl], e_pos[c - 1:c, sl]))
            outs = _wkv_pair_chunks(items, masks, c)
            for p in range(g):
                s_ref[p] = outs[p][1]
                y_ref[rows, p * PAIR:(p + 1) * PAIR] = outs[p][0]
            return carry

        lax.fori_loop(0, nreal, chunk, 0)

        @pl.when(i == pl.num_programs(2) - 1)
        def _():
            for p in range(g):
                s = s_ref[p]
                so_ref[0, 2 * p] = s[0:RWKV_HEAD, 0:RWKV_HEAD]
                so_ref[0, 2 * p + 1] = s[RWKV_HEAD:PAIR, RWKV_HEAD:PAIR]

    return kern


def _wkv_prompt(seqs, nb, tp, t_real, c=64, nchunk=4, g=8):
    d = seqs[0].shape[1]
    nh = d // RWKV_HEAD
    rb = c * nchunk
    nrb = tp // rb
    spec = pl.BlockSpec((rb, g * PAIR), lambda n, q, i: (n * nrb + i, q))
    return pl.pallas_call(
        _make_wkv_prompt_kernel(c, nchunk, g, t_real),
        grid=(nb, nh // (2 * g), nrb),
        in_specs=[spec] * 6,
        out_specs=[spec, pl.BlockSpec((1, 2 * g, RWKV_HEAD, RWKV_HEAD), lambda n, q, i: (n, q, 0, 0))],
        out_shape=[jax.ShapeDtypeStruct((nb * tp, d), F32),
                   jax.ShapeDtypeStruct((nb, nh, RWKV_HEAD, RWKV_HEAD), F32)],
        scratch_shapes=[pltpu.VMEM((g, PAIR, PAIR), F32)],
        compiler_params=_params(("parallel", "parallel", "arbitrary")),
        name="wkv_prompt",
    )(*seqs)


def _make_wkv_sample_kernel(c, ns, g):
    def kern(r_ref, lw_ref, k_ref, v_ref, kk_ref, b_ref, s0_ref, y_ref, so_ref):
        masks = _wkv_masks(c)
        row = lax.broadcasted_iota(jnp.int32, (c, g * PAIR), 0)
        items = []
        for n in range(ns):
            rows = slice(n * c, (n + 1) * c)
            lw = lw_ref[rows, :]
            cs = jnp.zeros(lw.shape, F32)
            for j in range(c):
                cs = cs + jnp.where(row >= j, lw[j:j + 1, :], 0.0)
            e_pos = jnp.exp(cs)
            e_neg = jnp.exp(-cs)
            rt = r_ref[rows, :] * e_pos
            qh = kk_ref[rows, :] * jnp.exp(cs - lw)
            bt = b_ref[rows, :] * e_neg
            kt = k_ref[rows, :] * e_neg
            vv = v_ref[rows, :]
            for p in range(g):
                sl = slice(p * PAIR, (p + 1) * PAIR)
                s = _pair_state(s0_ref[n, 2 * p], s0_ref[n, 2 * p + 1])
                items.append((s, rt[:, sl], qh[:, sl], bt[:, sl], kt[:, sl], vv[:, sl], e_pos[c - 1:c, sl]))
        outs = _wkv_pair_chunks(items, masks, c)
        for n in range(ns):
            for p in range(g):
                y, s = outs[n * g + p]
                y_ref[n * c:(n + 1) * c, p * PAIR:(p + 1) * PAIR] = y
                so_ref[n, 2 * p] = s[0:RWKV_HEAD, 0:RWKV_HEAD]
                so_ref[n, 2 * p + 1] = s[RWKV_HEAD:PAIR, RWKV_HEAD:PAIR]

    return kern


def _wkv_sample(seqs, s0, c, ns=4, g=4):
    d = seqs[0].shape[1]
    n = s0.shape[0]
    nh = d // RWKV_HEAD
    spec = pl.BlockSpec((ns * c, g * PAIR), lambda i, q: (i, q))
    st = pl.BlockSpec((ns, 2 * g, RWKV_HEAD, RWKV_HEAD), lambda i, q: (i, q, 0, 0))
    return pl.pallas_call(
        _make_wkv_sample_kernel(c, ns, g),
        grid=(n // ns, nh // (2 * g)),
        in_specs=[spec] * 6 + [st],
        out_specs=[spec, st],
        out_shape=[jax.ShapeDtypeStruct((n * c, d), F32),
                   jax.ShapeDtypeStruct((n, nh, RWKV_HEAD, RWKV_HEAD), F32)],
        compiler_params=_params(("parallel", "parallel")),
        name="wkv_sample",
    )(*seqs, s0)


def _rwkv_post_kernel(y_ref, r_ref, k_ref, v_ref, g_ref, lng_ref, lnb_ref, rk_ref, e_ref, et_ref, o_ref):
    e = e_ref[...]
    et = et_ref[...]
    inv = 1.0 / RWKV_HEAD
    y = y_ref[...]
    mu = _dot2(_dot2(y, e), et) * inv
    dlt = y - mu
    var = _dot2(_dot2(dlt * dlt, e), et) * inv
    yn = dlt * lax.rsqrt(var + RWKV_GN_EPS) * lng_ref[...] + lnb_ref[...]
    v = v_ref[...]
    bonus = _dot2(_dot2(r_ref[...] * k_ref[...] * rk_ref[...], e), et) * v
    o_ref[...] = ((yn + bonus) * g_ref[...]).astype(BF16)


def _rwkv_post(y, r, k2, v, g, prm, tr=128):
    nrows, d = y.shape
    e64, et64 = _seg_matrix(d, RWKV_HEAD)
    row = pl.BlockSpec((tr, d), lambda i: (i, 0))
    vec = pl.BlockSpec((1, d), lambda i: (0, 0))
    return pl.pallas_call(
        _rwkv_post_kernel,
        grid=(nrows // tr,),
        in_specs=[row] * 5 + [vec] * 3 + [pl.BlockSpec(e64.shape, lambda i: (0, 0)),
                                          pl.BlockSpec(et64.shape, lambda i: (0, 0))],
        out_specs=row,
        out_shape=jax.ShapeDtypeStruct((nrows, d), BF16),
        compiler_params=_params(("parallel",)),
        name="rwkv_post",
    )(y, r, k2, v, g, prm["ln_g"], prm["ln_b"], prm["r_k"], e64, et64)


def _make_mla_prep_kernel(prompt, nh):
    inv_dim = 1.0 / QK_DIM

    def kern(*refs):
        (qn_ref, q1_ref, q2_ref, c_ref, kp_ref, cos_ref, sin_ref, c128_ref, s128_ref,
         gqn_ref, gqr_ref, gkv_ref, e128_ref, et128_ref, e32_ref, et32_ref, p1_ref, p2_ref) = refs[:18]
        if prompt:
            (gkn_ref, gkr_ref, wuk_ref, wuv_ref, qo_ref, co_ref, ko_ref, kout_ref, vout_ref) = refs[18:]
        else:
            (qo_ref, co_ref, ko_ref) = refs[18:]
        scale = QK_DIM ** -0.5
        qn = qn_ref[...]
        cos = cos_ref[...]
        sin = sin_ref[...]
        q1 = q1_ref[...]
        q2 = q2_ref[...]
        r1 = q1 * cos - q2 * sin
        r2 = q2 * cos + q1 * sin
        ssq = _dot2(qn * qn, e128_ref[...]) + _dot2(r1 * r1, e32_ref[...]) + _dot2(r2 * r2, e32_ref[...])
        rf = lax.rsqrt(ssq * inv_dim + NORM_EPS) * scale
        qn = qn * _dot2(rf, et128_ref[...]) * gqn_ref[...]
        rf32 = _dot2(rf, et32_ref[...]) * gqr_ref[...]
        r1 = (r1 * rf32).astype(BF16)
        r2 = (r2 * rf32).astype(BF16)
        qr = (jnp.dot(r1, p1_ref[...], preferred_element_type=F32)
              + jnp.dot(r2, p2_ref[...], preferred_element_type=F32)).astype(BF16)
        qnb = qn.astype(BF16)
        for h in range(nh):
            qo_ref[:, h * QK_PAD:h * QK_PAD + QK_NOPE] = qnb[:, h * QK_NOPE:(h + 1) * QK_NOPE]
            qo_ref[:, h * QK_PAD + QK_NOPE:(h + 1) * QK_PAD] = qr[:, h * 128:(h + 1) * 128]
        c = c_ref[...]
        c = c * lax.rsqrt(jnp.mean(c * c, axis=-1, keepdims=True) + NORM_EPS) * gkv_ref[...]
        co_ref[...] = c
        kp = kp_ref[...]
        lane = lax.broadcasted_iota(jnp.int32, kp.shape, 1)
        swap = jnp.where(lane < QK_ROPE // 2, pltpu.roll(kp, 128 - QK_ROPE // 2, axis=1),
                         pltpu.roll(kp, QK_ROPE // 2, axis=1))
        kr = kp * c128_ref[...] + swap * s128_ref[...]
        ko_ref[...] = kr[:, :QK_ROPE]
        if prompt:
            cb = c.astype(BF16)
            kn = jnp.dot(cb, wuk_ref[...], preferred_element_type=F32)
            ssqk = _dot2(kn * kn, e128_ref[...]) + jnp.sum(kr * kr, axis=-1, keepdims=True)
            rk = _dot2(lax.rsqrt(ssqk * inv_dim + NORM_EPS), et128_ref[...])
            knb = (kn * rk * gkn_ref[...]).astype(BF16)
            krg = kr * gkr_ref[...]
            krb = (jnp.concatenate([krg] * nh, axis=1) * rk).astype(BF16)
            for h in range(nh):
                kout_ref[:, h * QK_PAD:h * QK_PAD + QK_NOPE] = knb[:, h * QK_NOPE:(h + 1) * QK_NOPE]
                kout_ref[:, h * QK_PAD + QK_NOPE:(h + 1) * QK_PAD] = krb[:, h * 128:(h + 1) * 128]
            vb = jnp.dot(cb, wuv_ref[...], preferred_element_type=F32).astype(BF16)
            ones = jnp.ones((vb.shape[0], V_HEAD), BF16)
            for h in range(nh):
                vout_ref[:, 2 * h * V_HEAD:(2 * h + 1) * V_HEAD] = vb[:, h * V_HEAD:(h + 1) * V_HEAD]
                vout_ref[:, (2 * h + 1) * V_HEAD:(2 * h + 2) * V_HEAD] = ones

    return kern


def _mla_prep(cols, lay, row0, nrows, prompt, tabs, prm, tr=256):
    nh = lay["mla_heads"]
    rb0 = row0 // tr
    cb = lay["cb"]
    dq = nh * QK_NOPE
    dr = nh * (QK_ROPE // 2)
    kvl = prm["kv_g"].shape[1]

    def colspec(width, blk):
        return pl.BlockSpec((tr, width), lambda i: (rb0 + i, blk))

    def tabspec(width):
        return pl.BlockSpec((tr, width), lambda i: (rb0 + i, 0))

    def full(a_):
        return pl.BlockSpec(a_.shape, lambda i: (0,) * a_.ndim)

    e128, et128 = _seg_matrix(dq, QK_NOPE)
    e32, et32 = _seg_matrix(dr, QK_ROPE // 2)
    idx = np.arange(dr)
    p1 = np.zeros((dr, nh * 128), np.float32)
    p2 = np.zeros((dr, nh * 128), np.float32)
    p1[idx, (idx // 32) * 128 + idx % 32] = 1.0
    p2[idx, (idx // 32) * 128 + 32 + idx % 32] = 1.0
    p1 = jnp.asarray(p1, BF16)
    p2 = jnp.asarray(p2, BF16)

    args = [cols, cols, cols, cols, cols, tabs["cos"], tabs["sin"], tabs["c128"], tabs["s128"],
            prm["gq_n"], prm["gq_r"], prm["kv_g"], e128, et128, e32, et32, p1, p2]
    in_specs = [colspec(dq, cb["q_nope"]), colspec(dr, cb["q_r1"]), colspec(dr, cb["q_r2"]),
                colspec(kvl, cb["c"]), colspec(128, cb["kpe"]),
                tabspec(dr), tabspec(dr), tabspec(128), tabspec(128)]
    in_specs += [full(a_) for a_ in args[9:]]
    row = lambda w_: pl.BlockSpec((tr, w_), lambda i: (i, 0))
    out_specs = [row(nh * QK_PAD), row(kvl), row(QK_ROPE)]
    out_shape = [jax.ShapeDtypeStruct((nrows, nh * QK_PAD), BF16),
                 jax.ShapeDtypeStruct((nrows, kvl), F32),
                 jax.ShapeDtypeStruct((nrows, QK_ROPE), F32)]
    if prompt:
        extra = [prm["gk_n"], prm["gk_r128"], prm["w_uk"], prm["w_uv"]]
        args += extra
        in_specs += [full(a_) for a_ in extra]
        out_specs += [row(nh * QK_PAD), row(nh * 2 * V_HEAD)]
        out_shape += [jax.ShapeDtypeStruct((nrows, nh * QK_PAD), BF16),
                      jax.ShapeDtypeStruct((nrows, nh * 2 * V_HEAD), BF16)]
    return pl.pallas_call(
        _make_mla_prep_kernel(prompt, nh),
        grid=(nrows // tr,),
        in_specs=in_specs,
        out_specs=out_specs,
        out_shape=out_shape,
        compiler_params=_params(("parallel",)),
        name="mla_prep_prompt" if prompt else "mla_prep_sample",
    )(*args)


def _make_prompt_attn_kernel(tile, hb):
    def kern(q_ref, k_ref, v_ref, o_ref, m_ref, acc_ref):
        qi = pl.program_id(2)
        m_ref[...] = jnp.full(m_ref.shape, NEG_INF, F32)
        acc_ref[...] = jnp.zeros(acc_ref.shape, F32)
        heads = range(hb)
        qs = [q_ref[:, h * QK_PAD:(h + 1) * QK_PAD] for h in heads]
        causal = (lax.broadcasted_iota(jnp.int32, (tile, tile), 1) <= lax.broadcasted_iota(jnp.int32, (tile, tile), 0))

        def kv_tile(j, diagonal):
            rows = pl.ds(pl.multiple_of(j * tile, tile), tile)
            ss = [lax.dot_general(qs[h], k_ref[rows, h * QK_PAD:(h + 1) * QK_PAD], _NT,
                                  preferred_element_type=F32) for h in heads]
            if diagonal:
                ss = [jnp.where(causal, s, NEG_INF) for s in ss]
            ms = [m_ref[h] for h in heads]
            mn = [jnp.maximum(ms[h], jnp.max(ss[h], axis=1, keepdims=True)) for h in heads]
            ps = [jnp.exp(ss[h] - jnp.concatenate([mn[h]] * (tile // 128), axis=1)) for h in heads]
            pv = [jnp.dot(ps[h].astype(BF16), v_ref[rows, h * 2 * V_HEAD:(h + 1) * 2 * V_HEAD],
                          preferred_element_type=F32) for h in heads]
            for h in heads:
                alpha = jnp.exp(ms[h] - mn[h])
                acc_ref[h] = jnp.concatenate([alpha, alpha], axis=1) * acc_ref[h] + pv[h]
                m_ref[h] = mn[h]

        def body(j, carry):
            kv_tile(j, False)
            return carry

        lax.fori_loop(0, qi, body, 0)
        kv_tile(qi, True)
        for h in heads:
            acc = acc_ref[h]
            o_ref[:, h * V_HEAD:(h + 1) * V_HEAD] = (acc[:, :V_HEAD] / acc[:, V_HEAD:]).astype(BF16)

    return kern


def _prompt_attn(q, k, v, nb, tp, nh, tile=ATTN_TILE, hb=4):
    nq = tp // tile
    return pl.pallas_call(
        _make_prompt_attn_kernel(tile, hb),
        grid=(nb, nh // hb, nq),
        in_specs=[pl.BlockSpec((tile, hb * QK_PAD), lambda b, h, i: (b * nq + i, h)),
                  pl.BlockSpec((tp, hb * QK_PAD), lambda b, h, i: (b, h)),
                  pl.BlockSpec((tp, hb * 2 * V_HEAD), lambda b, h, i: (b, h))],
        out_specs=pl.BlockSpec((tile, hb * V_HEAD), lambda b, h, i: (b * nq + i, h)),
        out_shape=jax.ShapeDtypeStruct((nb * tp, nh * V_HEAD), BF16),
        scratch_shapes=[pltpu.VMEM((hb, tile, 128), F32), pltpu.VMEM((hb, tile, 2 * V_HEAD), F32)],
        compiler_params=_params(("parallel", "parallel", "arbitrary")),
        name="prompt_attn",
    )(q, k, v)


def _sample_q_kernel(q_ref, gkn_ref, gkr_ref, wuk_ref, qa_ref, qr_ref):
    q = q_ref[...].astype(F32)
    qn = (q[:, :QK_NOPE] * gkn_ref[...]).astype(BF16)
    qa_ref[...] = lax.dot_general(qn, wuk_ref[...], (((1,), (1,)), ((), ())),
                                  preferred_element_type=F32).astype(BF16)
    qr_ref[...] = (q[:, QK_NOPE:] * gkr_ref[...]).astype(BF16)


def _sample_q(q, prm, nh):
    rs = q.shape[0]
    kvl = prm["w_uk"].shape[0]
    return pl.pallas_call(
        _sample_q_kernel,
        grid=(nh,),
        in_specs=[pl.BlockSpec((rs, QK_PAD), lambda h: (0, h)),
                  pl.BlockSpec((1, QK_NOPE), lambda h: (0, 0)),
                  pl.BlockSpec((1, 128), lambda h: (0, 0)),
                  pl.BlockSpec((kvl, QK_NOPE), lambda h: (0, h))],
        out_specs=[pl.BlockSpec((rs, kvl), lambda h: (0, h)),
                   pl.BlockSpec((rs, 128), lambda h: (0, h))],
        out_shape=[jax.ShapeDtypeStruct((rs, nh * kvl), BF16),
                   jax.ShapeDtypeStruct((rs, nh * 128), BF16)],
        compiler_params=_params(("parallel",)),
        name="sample_q",
    )(q, prm["gk_n1"], prm["gk_r128"], prm["w_uk"])


SA_SLOTS = 16
SA_UNROLL = 4


def _make_sample_attn_kernel(nh, ds, kvl, page, n_pages, chunk_pages):
    nq = nh * ds
    nw = nh * QK_NOPE
    ct = chunk_pages * page
    nch = n_pages // chunk_pages
    inv_dim = 1.0 / QK_DIM

    new = SA_SLOTS
    pre = SA_SLOTS - SA_UNROLL
    assert nch % SA_SLOTS == 0 and nch % SA_UNROLL == 0 and SA_UNROLL % 2 == 0 and pre > SA_UNROLL

    def kern(pt_ref, wt_ref, qa_ref, qr_ref, cnew_ref, rnew_ref, cc_ref, cr_ref, o_ref,
             l_ref, cbuf, rbuf, kt0, kt1, a20, a21, sem):
        n = pl.program_id(0)
        nseq = pl.num_programs(0)

        def copies(seq, j, slot):
            out = []
            for p in range(chunk_pages):
                pg = pt_ref[seq * n_pages + j * chunk_pages + p]
                out.append(pltpu.make_async_copy(cc_ref.at[pg], cbuf.at[slot, pl.ds(p * page, page)],
                                                 sem.at[slot, 2 * p]))
                out.append(pltpu.make_async_copy(cr_ref.at[pg], rbuf.at[slot, :, pl.ds(p * page, page)],
                                                 sem.at[slot, 2 * p + 1]))
            return out

        def request(t):
            seq2 = n + t // nch
            ch2 = t % nch

            @pl.when(seq2 < nseq)
            def _():
                for cp in copies(seq2, ch2, ch2 % SA_SLOTS):
                    cp.start()

        def arrive(j):
            @pl.when(j < nch)
            def _():
                for cp in copies(n, j, j % SA_SLOTS):
                    cp.wait()

        @pl.when(n == 0)
        def _():
            l_ref[0:nw, :] = wt_ref[...]
            for j in range(pre):
                for cp in copies(0, j, j):
                    cp.start()

        l_ref[nw:nw + nq, :] = qa_ref[0]
        cbuf[new] = jnp.zeros((ct, kvl), F32)
        rbuf[new] = jnp.zeros((QK_ROPE, ct), F32)
        cbuf[new, 0:8, :] = cnew_ref[0]
        rbuf[new, :, 0:128] = rnew_ref[0]

        def scores(j, kt_ref, a2_ref):
            slot = jnp.where(j < nch, j % SA_SLOTS, new)
            cb = cbuf[slot].astype(BF16)
            rb = rbuf[slot].astype(BF16)
            kt_ref[...] = lax.dot_general(l_ref[...], cb, _NT, preferred_element_type=F32)
            a2_ref[...] = jnp.dot(qr_ref[0], rb, preferred_element_type=F32)

        def softmax_pv(slot, kt_ref, a2_ref, carry, causal):
            m, l, acc = carry
            kn = kt_ref[0:nw, :].reshape(nh, QK_NOPE, ct)
            ssq = jnp.sum(kn * kn, axis=1)
            kpt = rbuf[slot]
            ssq = ssq + jnp.sum(kpt * kpt, axis=0, keepdims=True)
            rf = lax.rsqrt(ssq * inv_dim + NORM_EPS)
            s = (kt_ref[nw:nw + nq, :] + a2_ref[...]) * jnp.concatenate([rf] * ds, axis=0)
            if causal:
                qidx = lax.broadcasted_iota(jnp.int32, (nq, ct), 0) // nh
                kidx = lax.broadcasted_iota(jnp.int32, (nq, ct), 1)
                s = jnp.where(kidx <= qidx, s, NEG_INF)
            m_new = jnp.maximum(m, jnp.max(s, axis=1, keepdims=True))
            alpha = jnp.exp(m - m_new)
            p = jnp.exp(s - m_new)
            l = alpha * l + jnp.sum(p, axis=1, keepdims=True)
            acc = alpha * acc + jnp.dot(p.astype(BF16), cbuf[slot].astype(BF16), preferred_element_type=F32)
            return m_new, l, acc

        arrive(0)
        scores(0, kt0, a20)

        def body(i, carry):
            j = SA_UNROLL * i
            for k_ in range(SA_UNROLL):
                request(j + pre + k_)
            for k_ in range(1, SA_UNROLL + 1):
                arrive(j + k_)
            bufs = ((kt0, a20), (kt1, a21))
            for k_ in range(SA_UNROLL):
                scores(j + k_ + 1, *bufs[(k_ + 1) % 2])
                carry = softmax_pv((j + k_) % SA_SLOTS, *bufs[k_ % 2], carry, False)
            return carry

        init = (jnp.full((nq, 1), NEG_INF, F32), jnp.zeros((nq, 1), F32), jnp.zeros((nq, kvl), F32))
        carry = lax.fori_loop(0, nch // SA_UNROLL, body, init)
        m, l, acc = softmax_pv(new, kt0, a20, carry, True)
        o_ref[0] = acc / l

    return kern


def _sample_attn(page_table, wt, qa, qr, cnew, rnew_t, cache_c, cache_rt, nh, ds, chunk_pages=2):
    db, n_pages = page_table.shape
    _, page, kvl = cache_c.shape
    nq = nh * ds
    nw = nh * QK_NOPE
    ct = chunk_pages * page
    assert page == 128
    grid_spec = pltpu.PrefetchScalarGridSpec(
        num_scalar_prefetch=1,
        grid=(db,),
        in_specs=[pl.BlockSpec((nw, kvl), lambda n, pt: (0, 0)),
                  pl.BlockSpec((1, nq, kvl), lambda n, pt: (n, 0, 0)),
                  pl.BlockSpec((1, nq, QK_ROPE), lambda n, pt: (n, 0, 0)),
                  pl.BlockSpec((1, 8, kvl), lambda n, pt: (n, 0, 0)),
                  pl.BlockSpec((1, QK_ROPE, 128), lambda n, pt: (n, 0, 0)),
                  pl.BlockSpec(memory_space=pl.ANY),
                  pl.BlockSpec(memory_space=pl.ANY)],
        out_specs=pl.BlockSpec((1, nq, kvl), lambda n, pt: (n, 0, 0)),
        scratch_shapes=[pltpu.VMEM((nw + nq, kvl), BF16),
                        pltpu.VMEM((SA_SLOTS + 1, ct, kvl), F32),
                        pltpu.VMEM((SA_SLOTS + 1, QK_ROPE, ct), F32),
                        pltpu.VMEM((nw + nq, ct), F32), pltpu.VMEM((nw + nq, ct), F32),
                        pltpu.VMEM((nq, ct), F32), pltpu.VMEM((nq, ct), F32),
                        pltpu.SemaphoreType.DMA((SA_SLOTS, 2 * chunk_pages))],
    )
    return pl.pallas_call(
        _make_sample_attn_kernel(nh, ds, kvl, page, n_pages, chunk_pages),
        grid_spec=grid_spec,
        out_shape=jax.ShapeDtypeStruct((db, nq, kvl), F32),
        compiler_params=_params(("arbitrary",)),
        name="sample_attn",
    )(page_table.reshape(-1), wt, qa, qr, cnew, rnew_t, cache_c, cache_rt)


def _sample_o_kernel(ol_ref, wuv_ref, o_ref):
    o_ref[...] = jnp.dot(ol_ref[...].astype(BF16), wuv_ref[...], preferred_element_type=F32).astype(BF16)


def _sample_o(olat, w_uv, nh):
    rs = olat.shape[0]
    kvl = w_uv.shape[0]
    return pl.pallas_call(
        _sample_o_kernel,
        grid=(nh,),
        in_specs=[pl.BlockSpec((rs, kvl), lambda h: (0, h)),
                  pl.BlockSpec((kvl, V_HEAD), lambda h: (0, h))],
        out_specs=pl.BlockSpec((rs, V_HEAD), lambda h: (0, h)),
        out_shape=jax.ShapeDtypeStruct((rs, nh * V_HEAD), BF16),
        compiler_params=_params(("parallel",)),
        name="sample_o",
    )(olat, w_uv)


def _merge_kernel(ya_ref, yb_ref, ga_ref, gb_ref, wa_ref, wb_ref, o_ref):
    pa = jnp.dot(ya_ref[...], wa_ref[...], preferred_element_type=F32)
    pb = jnp.dot(yb_ref[...], wb_ref[...], preferred_element_type=F32)
    o_ref[...] = (_sigmoid(ga_ref[...]) * pa + _sigmoid(gb_ref[...]) * pb).astype(BF16)


def _merge(ya, yb, cols, lay, wa, wb, tm=512, tn=512):
    r, d = ya.shape
    nj = d // tn
    ga0 = lay["cb"]["gate_a"] * (d // tn)
    gb0 = lay["cb"]["gate_b"] * (d // tn)
    return pl.pallas_call(
        _merge_kernel,
        grid=(nj, r // tm),
        in_specs=[pl.BlockSpec((tm, d), lambda j, i: (i, 0)),
                  pl.BlockSpec((tm, d), lambda j, i: (i, 0)),
                  pl.BlockSpec((tm, tn), lambda j, i: (i, ga0 + j)),
                  pl.BlockSpec((tm, tn), lambda j, i: (i, gb0 + j)),
                  pl.BlockSpec((d, tn), lambda j, i: (0, j)),
                  pl.BlockSpec((d, tn), lambda j, i: (0, j))],
        out_specs=pl.BlockSpec((tm, tn), lambda j, i: (i, j)),
        out_shape=jax.ShapeDtypeStruct((r, d), BF16),
        compiler_params=_params(("parallel", "parallel")),
        name="branch_merge",
    )(ya, yb, cols, cols, wa, wb)


def _make_out_router_kernel(n_experts):
    def kern(m_ref, x_ref, wo_ref, g2_ref, wrh_ref, wrl_ref, br_ref, x1_ref, h_ref, gate_ref, idx_ref):
        x1 = x_ref[...] + jnp.dot(m_ref[...], wo_ref[...], preferred_element_type=F32)
        x1_ref[...] = x1
        h = x1 * lax.rsqrt(jnp.mean(x1 * x1, axis=-1, keepdims=True) + NORM_EPS) * g2_ref[...]
        for c_ in range(h.shape[1] // 128):
            h_ref[pl.ds(c_, h.shape[0], stride=h.shape[1] // 128), :] = h[:, c_ * 128:(c_ + 1) * 128]
        hh, hl = _split(h)
        logits = (jnp.dot(hh, wrh_ref[...], preferred_element_type=F32)
                  + jnp.dot(hh, wrl_ref[...], preferred_element_type=F32)
                  + jnp.dot(hl, wrh_ref[...], preferred_element_type=F32)) + br_ref[...]
        lane = lax.broadcasted_iota(jnp.int32, logits.shape, 1)
        work = jnp.where(lane < n_experts, logits, -jnp.inf)
        vals, idxs = [], []
        for _ in range(TOP_K):
            mx = jnp.max(work, axis=1, keepdims=True)
            ix = jnp.min(jnp.where(work == mx, lane, 128), axis=1, keepdims=True)
            vals.append(mx)
            idxs.append(ix)
            work = jnp.where(lane == ix, -jnp.inf, work)
        ex = [jnp.exp(v_ - vals[0]) for v_ in vals]
        den = ex[0] + ex[1] + ex[2] + ex[3]
        gate = jnp.zeros(logits.shape, F32)
        idx = jnp.zeros(logits.shape, jnp.int32)
        for k_ in range(TOP_K):
            gate = jnp.where(lane == k_, ex[k_] / den, gate)
            idx = jnp.where(lane == k_, idxs[k_], idx)
        gate_ref[...] = gate
        idx_ref[...] = idx

    return kern


def _out_router(merged, x, wo, ln2, wr_hi, wr_lo, br, n_experts, tm=256):
    r, d = x.shape
    row = pl.BlockSpec((tm, d), lambda i: (i, 0))
    small = pl.BlockSpec((tm, 128), lambda i: (i, 0))
    full = lambda a_: pl.BlockSpec(a_.shape, lambda i: (0, 0))
    return pl.pallas_call(
        _make_out_router_kernel(n_experts),
        grid=(r // tm,),
        in_specs=[row, row, full(wo), full(ln2), full(wr_hi), full(wr_lo), full(br)],
        out_specs=[row, pl.BlockSpec((tm * (d // 128), 128), lambda i: (i, 0)), small, small],
        out_shape=[jax.ShapeDtypeStruct((r, d), F32), jax.ShapeDtypeStruct((r * (d // 128), 128), F32),
                   jax.ShapeDtypeStruct((r, 128), F32), jax.ShapeDtypeStruct((r, 128), jnp.int32)],
        compiler_params=_params(("parallel",)),
        name="out_proj_router",
    )(merged, x, wo, ln2, wr_hi, wr_lo, br)


def _make_gather_kernel(bm, nc):
    def kern(nused_ref, tok_ref, nxt_ref, h_ref, o_ref, buf, sem):
        b = pl.program_id(0)
        slot = b % 2

        def fetch(idx_ref, s):
            def issue(i, carry):
                src = pl.multiple_of(idx_ref[0, 0, i] * nc, nc)
                pltpu.make_async_copy(h_ref.at[pl.ds(src, nc)], buf.at[s, pl.ds(pl.multiple_of(i * nc, nc), nc)],
                                      sem.at[s]).start()
                return carry

            lax.fori_loop(0, bm, issue, 0, unroll=8)

        @pl.when((b == 0) & (nused_ref[0] > 0))
        def _():
            fetch(tok_ref, 0)

        @pl.when(b + 1 < nused_ref[0])
        def _():
            fetch(nxt_ref, 1 - slot)

        @pl.when(b < nused_ref[0])
        def _():
            pltpu.make_async_copy(h_ref.at[pl.ds(0, bm * nc)], buf.at[slot], sem.at[slot]).wait()
            for c_ in range(nc):
                o_ref[:, c_ * 128:(c_ + 1) * 128] = buf[slot, pl.ds(c_, bm, stride=nc), :].astype(BF16)

        @pl.when(b >= nused_ref[0])
        def _():
            o_ref[...] = jnp.zeros(o_ref.shape, BF16)

    return kern


def _moe_gather(h, tok_sorted, nused, bm, nc):
    mp = tok_sorted.shape[0]
    d = nc * 128
    nb = mp // bm
    tok3 = tok_sorted.reshape(nb, 1, bm)
    grid_spec = pltpu.PrefetchScalarGridSpec(
        num_scalar_prefetch=1,
        grid=(nb,),
        in_specs=[pl.BlockSpec((1, 1, bm), lambda b, nu: (b, 0, 0), memory_space=pltpu.SMEM),
                  pl.BlockSpec((1, 1, bm), lambda b, nu: (jnp.minimum(b + 1, nb - 1), 0, 0),
                               memory_space=pltpu.SMEM),
                  pl.BlockSpec(memory_space=pl.ANY)],
        out_specs=pl.BlockSpec((bm, d), lambda b, nu: (b, 0)),
        scratch_shapes=[pltpu.VMEM((2, bm * nc, 128), F32), pltpu.SemaphoreType.DMA((2,))],
    )
    return pl.pallas_call(
        _make_gather_kernel(bm, nc),
        grid_spec=grid_spec,
        out_shape=jax.ShapeDtypeStruct((mp, d), BF16),
        compiler_params=_params(("arbitrary",)),
        name="moe_gather",
    )(nused, tok3, tok3, h)


WK_E, WK_J, WK_B, WK_JO, WK_VALID, WK_FIRST = range(6)


def _moe_up_kernel(wk_ref, x_ref, wg_ref, wu_ref, bg_ref, bu_ref, o_ref, wgb, wub):
    w = pl.program_id(0)

    @pl.when(wk_ref[WK_VALID, w] == 1)
    def _():
        @pl.when(wk_ref[WK_FIRST, w] == 1)
        def _():
            wgb[...] = wg_ref[0].astype(BF16)
            wub[...] = wu_ref[0].astype(BF16)

        x = x_ref[...]
        hg = jnp.dot(x, wgb[...], preferred_element_type=F32) + bg_ref[0]
        hu = jnp.dot(x, wub[...], preferred_element_type=F32) + bu_ref[0]
        gt = jnp.minimum(hg, SWIGLU_LIMIT)
        up = jnp.clip(hu, -SWIGLU_LIMIT, SWIGLU_LIMIT)
        o_ref[...] = ((up + 1.0) * gt * _sigmoid(SWIGLU_ALPHA * gt)).astype(BF16)

    @pl.when(wk_ref[WK_VALID, w] == 0)
    def _():
        o_ref[...] = jnp.zeros(o_ref.shape, BF16)


def _moe_up(xs, w_gu, b_gu, work, bm, tf):
    mp, d = xs.shape
    ne, _, f2 = w_gu.shape
    dff = f2 // 2
    nj = dff // tf
    nwork = work.shape[1]
    grid_spec = pltpu.PrefetchScalarGridSpec(
        num_scalar_prefetch=1,
        grid=(nwork,),
        in_specs=[pl.BlockSpec((bm, d), lambda w, wk: (wk[WK_B, w], 0)),
                  pl.BlockSpec((1, d, tf), lambda w, wk: (wk[WK_E, w], 0, wk[WK_J, w])),
                  pl.BlockSpec((1, d, tf), lambda w, wk: (wk[WK_E, w], 0, nj + wk[WK_J, w])),
                  pl.BlockSpec((1, 1, tf), lambda w, wk: (wk[WK_E, w], 0, wk[WK_J, w])),
                  pl.BlockSpec((1, 1, tf), lambda w, wk: (wk[WK_E, w], 0, nj + wk[WK_J, w]))],
        out_specs=pl.BlockSpec((bm, tf), lambda w, wk: (wk[WK_B, w], wk[WK_JO, w])),
        scratch_shapes=[pltpu.VMEM((d, tf), BF16), pltpu.VMEM((d, tf), BF16)],
    )
    return pl.pallas_call(
        _moe_up_kernel,
        grid_spec=grid_spec,
        out_shape=jax.ShapeDtypeStruct((mp, dff), BF16),
        compiler_params=_params(("arbitrary",)),
        name="moe_up",
    )(work, xs, w_gu, w_gu, b_gu, b_gu)


def _moe_down_kernel(wk_ref, a_ref, wd_ref, bd_ref, o_ref, wdb):
    w = pl.program_id(0)

    @pl.when(wk_ref[WK_VALID, w] == 1)
    def _():
        @pl.when(wk_ref[WK_FIRST, w] == 1)
        def _():
            wdb[...] = wd_ref[0].astype(BF16)

        o_ref[...] = jnp.dot(a_ref[...], wdb[...], preferred_element_type=F32) + bd_ref[0]

    @pl.when(wk_ref[WK_VALID, w] == 0)
    def _():
        o_ref[...] = jnp.zeros(o_ref.shape, F32)


def _moe_down(act, w_down, b_down, work, bm, tn):
    mp, dff = act.shape
    ne, _, d = w_down.shape
    nwork = work.shape[1]
    grid_spec = pltpu.PrefetchScalarGridSpec(
        num_scalar_prefetch=1,
        grid=(nwork,),
        in_specs=[pl.BlockSpec((bm, dff), lambda w, wk: (wk[WK_B, w], 0)),
                  pl.BlockSpec((1, dff, tn), lambda w, wk: (wk[WK_E, w], 0, wk[WK_J, w])),
                  pl.BlockSpec((1, 1, tn), lambda w, wk: (wk[WK_E, w], 0, wk[WK_J, w]))],
        out_specs=pl.BlockSpec((bm, tn), lambda w, wk: (wk[WK_B, w], wk[WK_JO, w])),
        scratch_shapes=[pltpu.VMEM((dff, tn), BF16)],
    )
    return pl.pallas_call(
        _moe_down_kernel,
        grid_spec=grid_spec,
        out_shape=jax.ShapeDtypeStruct((mp, d), F32),
        compiler_params=_params(("arbitrary",)),
        name="moe_down",
    )(work, act, w_down, b_down)


def _make_combine_kernel(tm):
    def kern(pos_ref, nxt_ref, x_ref, g_ref, ys_ref, o_ref, buf, sem):
        i = pl.program_id(0)
        slot = i % 2

        def fetch(idx_ref, s):
            def issue(r, carry):
                for k_ in range(TOP_K):
                    pltpu.make_async_copy(ys_ref.at[pl.ds(idx_ref[0, 0, r * TOP_K + k_], 1)],
                                          buf.at[s, k_, pl.ds(r, 1)], sem.at[s]).start()
                return carry

            lax.fori_loop(0, tm, issue, 0, unroll=4)

        @pl.when(i == 0)
        def _():
            fetch(pos_ref, 0)

        @pl.when(i + 1 < pl.num_programs(0))
        def _():
            fetch(nxt_ref, 1 - slot)

        for k_ in range(TOP_K):
            pltpu.make_async_copy(ys_ref.at[pl.ds(0, tm)], buf.at[slot, k_], sem.at[slot]).wait()
        g = g_ref[...]
        o_ref[...] = x_ref[...] + ((buf[slot, 0] * g[:, 0:1] + buf[slot, 1] * g[:, 1:2])
                                   + (buf[slot, 2] * g[:, 2:3] + buf[slot, 3] * g[:, 3:4]))

    return kern


def _moe_combine(x1, ys, pos, gate, tm=128):
    r, d = x1.shape
    nt = r // tm
    pos3 = pos.reshape(nt, 1, tm * TOP_K)
    return pl.pallas_call(
        _make_combine_kernel(tm),
        grid=(nt,),
        in_specs=[pl.BlockSpec((1, 1, tm * TOP_K), lambda i: (i, 0, 0), memory_space=pltpu.SMEM),
                  pl.BlockSpec((1, 1, tm * TOP_K), lambda i: (jnp.minimum(i + 1, nt - 1), 0, 0),
                               memory_space=pltpu.SMEM),
                  pl.BlockSpec((tm, d), lambda i: (i, 0)),
                  pl.BlockSpec((tm, 128), lambda i: (i, 0)),
                  pl.BlockSpec(memory_space=pl.ANY)],
        out_specs=pl.BlockSpec((tm, d), lambda i: (i, 0)),
        out_shape=jax.ShapeDtypeStruct((r, d), F32),
        scratch_shapes=[pltpu.VMEM((2, TOP_K, tm, d), F32), pltpu.SemaphoreType.DMA((2,))],
        compiler_params=_params(("arbitrary",)),
        name="moe_combine",
    )(pos3, pos3, x1, gate, ys)


def _moe_routing(idx, segments, n_rows, n_experts, bm, nj):
    n_real = sum(n_ for _, n_ in segments)
    m = n_real * TOP_K
    nb_total = -(-m // bm) + n_experts
    mp = nb_total * bm
    e_flat = jnp.concatenate([idx[s_:s_ + n_] for s_, n_ in segments], axis=0).reshape(m)
    order = jnp.argsort(e_flat)
    rank_all = jnp.argsort(order)
    ex = jnp.arange(n_experts, dtype=jnp.int32)
    onehot = (e_flat[:, None] == ex[None, :]).astype(jnp.int32)
    counts = jnp.sum(onehot, axis=0)
    starts = jnp.cumsum(counts) - counts
    nblk = (counts + bm - 1) // bm
    blk_ends = jnp.cumsum(nblk)
    blk_starts = blk_ends - nblk
    nused = blk_ends[-1]
    base = jnp.sum(onehot * (blk_starts * bm - starts)[None, :], axis=1)
    slot = (base + rank_all).astype(jnp.int32).reshape(n_real, TOP_K)
    pieces, at, off = [], 0, 0
    for s_, n_ in segments:
        pieces += [jnp.zeros((s_ - at, TOP_K), jnp.int32), slot[off:off + n_]]
        at, off = s_ + n_, off + n_
    pos = jnp.concatenate(pieces + [jnp.zeros((n_rows - at, TOP_K), jnp.int32)], axis=0)
    blk_e = jnp.minimum(jnp.sum((blk_ends[None, :] <= jnp.arange(nb_total, dtype=jnp.int32)[:, None])
                                .astype(jnp.int32), axis=1), n_experts - 1)
    blk_first = starts[blk_e] + (jnp.arange(nb_total, dtype=jnp.int32) - blk_starts[blk_e]) * bm
    blk_real = jnp.clip(starts[blk_e] + counts[blk_e] - blk_first, 0, bm)
    lane = jnp.arange(bm, dtype=jnp.int32)[None, :]
    src = order[jnp.clip(blk_first[:, None] + lane, 0, m - 1).reshape(mp)] // TOP_K
    tok, cum, prev_shift = src, 0, 0
    for s_, n_ in segments:
        tok = tok + jnp.where(src >= cum, (s_ - cum) - prev_shift, 0)
        prev_shift, cum = s_ - cum, cum + n_
    tok_sorted = jnp.where(lane < blk_real[:, None], tok.reshape(nb_total, bm), 0).reshape(mp).astype(jnp.int32)
    w = jnp.arange(nb_total * nj, dtype=jnp.int32)
    valid = w < nused * nj
    wc = jnp.minimum(w, jnp.maximum(nused * nj - 1, 0))
    we = blk_e[wc // nj]
    local = wc - nj * blk_starts[we]
    nbe = jnp.maximum(nblk[we], 1)
    wj = local // nbe
    wf = ((local % nbe) == 0) & valid
    wb = jnp.where(valid, blk_starts[we] + local % nbe, w // nj)
    wjo = jnp.where(valid, wj, w % nj)
    work = jnp.stack([we, wj, wb, wjo, valid.astype(jnp.int32), wf.astype(jnp.int32)]).astype(jnp.int32)
    return tok_sorted, pos, nused.reshape(1).astype(jnp.int32), work


def _layout(d, nh_mla, kvl, n_w, n_a, n_g):
    assert d == 2048 and nh_mla * QK_NOPE == d and kvl == 512 and n_g == 256 and n_w <= 128 and n_a <= 128
    cb = {"q_nope": 0, "q_r1": 4, "q_r2": 5, "c": 6, "kpe": 28, "gate_a": 2, "gate_b": 3,
          "r": 4, "k": 5, "v": 6, "lora": 28}
    return {"d": d, "mla_heads": nh_mla, "cb": cb, "nc": 14848}


def _relayout_w_in(w, d, nh, kvl, n_w, n_a, n_g):
    o_w = 3 * d
    o_a = o_w + n_w
    o_g = o_a + n_a
    c1 = o_g + n_g
    c2 = c1 + nh * QK_DIM
    c3 = c2 + kvl
    c4 = c3 + QK_ROPE
    q = w[:, c1:c2].reshape(d, nh, QK_DIM)
    z = lambda n_: jnp.zeros((d, n_), w.dtype)
    half = QK_ROPE // 2
    parts = [q[:, :, :QK_NOPE].reshape(d, nh * QK_NOPE),
             q[:, :, QK_NOPE:QK_NOPE + half].reshape(d, nh * half),
             q[:, :, QK_NOPE + half:].reshape(d, nh * half),
             w[:, c2:c3], w[:, c3:c4], z(128 - QK_ROPE), z(384),
             w[:, c4:c4 + d], w[:, c4 + d:c4 + 2 * d],
             w[:, 0:3 * d],
             w[:, o_w:o_a], z(128 - n_w), w[:, o_a:o_g], z(128 - n_a), w[:, o_g:c1]]
    return jnp.concatenate(parts, axis=1).astype(BF16)


def _pad_lora_vec(vec, d, n_w, n_a, n_g):
    o_w = 3 * d
    o_a = o_w + n_w
    o_g = o_a + n_a
    z = lambda n_: jnp.zeros(vec.shape[:-1] + (n_,), vec.dtype)
    lora = jnp.concatenate([vec[..., o_w:o_a], z(128 - n_w), vec[..., o_a:o_g], z(128 - n_a),
                            vec[..., o_g:o_g + n_g]], axis=-1)
    return vec[..., 0:d], vec[..., d:2 * d], vec[..., 2 * d:3 * d], lora


def _unpad_shift(cols_row, lay, d, n_w, n_a, n_g):
    rkv = cols_row[..., 4 * d:7 * d]
    lo = cols_row[..., 7 * d:7 * d + 512]
    return jnp.concatenate([rkv, lo[..., 0:n_w], lo[..., 128:128 + n_a], lo[..., 256:256 + n_g]], axis=-1)


def _rope_tables(pos, nh):
    half = QK_ROPE // 2
    inv = ROPE_THETA ** (-jnp.arange(half, dtype=F32) / half)
    ang = pos.astype(F32)[:, None] * inv[None, :]
    cos, sin = jnp.cos(ang), jnp.sin(ang)
    z = jnp.zeros((pos.shape[0], 128 - QK_ROPE), F32)
    return {"cos": jnp.tile(cos, (1, nh)), "sin": jnp.tile(sin, (1, nh)),
            "c128": jnp.concatenate([cos, cos, z], axis=1),
            "s128": jnp.concatenate([-sin, sin, z], axis=1)}


def kernel(x_prompt, x_sample, cache_kv_latent, cache_k_rope, state_wkv, state_shift, page_table, meta_tokens,
           ln1_g, w_in, rw_mu, rw_w0, rw_w2, rw_a0, rw_a2, rw_g2, rw_k_k, rw_k_a, rw_r_k, rw_ln_g, rw_ln_b,
           q_norm_g, k_norm_g, kv_norm_g, w_kv_up, w_branch_a, w_branch_b, w_out, ln2_g,
           w_router, b_router, w_gu, b_gu, w_down, b_down):
    depth = w_in.shape[0]
    nb, seq, d = x_prompt.shape
    db, ds, _ = x_sample.shape
    n_meta = meta_tokens.shape[0]
    t = seq + n_meta
    tp = -(-t // ATTN_TILE) * ATTN_TILE
    rp = nb * tp
    rs = db * ds
    r_all = rp + rs
    n_pages = page_table.shape[1]
    page = cache_kv_latent.shape[2]
    kvl = cache_kv_latent.shape[3]
    past = n_pages * page
    nh_rw = d // RWKV_HEAD
    n_w, n_a, n_g = rw_w2.shape[1], rw_a2.shape[1], rw_g2.shape[1]
    nh = w_kv_up.shape[2] // (QK_NOPE + V_HEAD)
    n_experts = w_router.shape[2]
    lay = _layout(d, nh, kvl, n_w, n_a, n_g)
    assert r_all % 1024 == 0 and rs % 256 == 0 and ds <= 8

    meta = meta_tokens.astype(x_prompt.dtype)
    zpad = jnp.zeros((tp - t, d), x_prompt.dtype)
    x = jnp.concatenate([a_ for b_ in range(nb) for a_ in (meta, x_prompt[b_], zpad)] + [x_sample.reshape(rs, d)],
                        axis=0)

    pos_all = jnp.concatenate([jnp.tile(jnp.arange(tp), nb), jnp.tile(past + jnp.arange(ds), db)])
    tabs = _rope_tables(pos_all, nh)
    segments = [(b_ * tp, t) for b_ in range(nb)] + [(rp, rs)]

    outs = {k_: [] for k_ in ("lat_p", "kpe_p", "wkv_p", "sh_p", "lat_s", "kpe_s", "wkv_s", "sh_s")}
    for l in range(depth):
        w_in_l = _relayout_w_in(w_in[l], d, nh, kvl, n_w, n_a, n_g)
        cols = _in_proj(x, ln1_g[l][None], w_in_l)

        mu_r, mu_k, mu_v, mu_l = _pad_lora_vec(rw_mu[l][None], d, n_w, n_a, n_g)
        zrow = lambda a_, n_: jnp.concatenate([a_, jnp.zeros((128 - n_, d), a_.dtype)], axis=0)
        rprm = {"mu_r": mu_r, "mu_k": mu_k, "mu_v": mu_v, "mu_l": mu_l,
                "w0": rw_w0[l][None], "w2": zrow(rw_w2[l], n_w), "a0": rw_a0[l][None], "a2": zrow(rw_a2[l], n_a),
                "g2": rw_g2[l], "k_k": rw_k_k[l][None], "k_a": rw_k_a[l][None],
                "ln_g": rw_ln_g[l][None], "ln_b": rw_ln_b[l][None], "r_k": rw_r_k[l].reshape(1, d)}
        pp = _rwkv_prep(cols, lay, 0, rp, False, None, rprm, tp // 128, t, ds)
        sh = jnp.zeros((db, ds, state_shift.shape[2]), F32).at[:, 0].set(state_shift[l]).reshape(rs, -1)
        sp = _rwkv_prep(cols, lay, rp, rs, True, _pad_lora_vec(sh, d, n_w, n_a, n_g), rprm, 1, ds, ds)
        r_p, lw_p, k_p, v_p, kk_p, b_p, g_p = pp
        r_s, lw_s, k_s, v_s, kk_s, b_s, g_s = sp
        y_p, wkv_p = _wkv_prompt([r_p, lw_p, k_p, v_p, kk_p, b_p], nb, tp, t)
        seq_s = [jnp.pad(a_.reshape(db, ds, d), ((0, 0), (0, 8 - ds), (0, 0))).reshape(db * 8, d)
                 for a_ in (r_s, lw_s, k_s, v_s, kk_s, b_s)]
        y_s, wkv_s = _wkv_sample(seq_s, state_wkv[l], 8)
        y_s = y_s.reshape(db, 8, d)[:, :ds].reshape(rs, d)
        ya = jnp.concatenate([_rwkv_post(y_p, r_p, k_p, v_p, g_p, rprm),
                              _rwkv_post(y_s, r_s, k_s, v_s, g_s, rprm)], axis=0)

        w_up = w_kv_up[l].reshape(kvl, nh, QK_NOPE + V_HEAD)
        gq, gk = q_norm_g[l], k_norm_g[l]
        half = QK_ROPE // 2
        z64 = jnp.zeros((128 - QK_ROPE,), F32)
        mprm = {"gq_n": jnp.tile(gq[:QK_NOPE], nh)[None], "gq_r": jnp.tile(gq[QK_NOPE:], nh)[None],
                "gk_n": jnp.tile(gk[:QK_NOPE], nh)[None], "gk_n1": gk[:QK_NOPE][None],
                "gk_r128": jnp.concatenate([gk[QK_NOPE:], gk[QK_NOPE:], z64])[None],
                "kv_g": kv_norm_g[l][None],
                "w_uk": w_up[:, :, :QK_NOPE].reshape(kvl, nh * QK_NOPE).astype(BF16),
                "w_uv": w_up[:, :, QK_NOPE:].reshape(kvl, nh * V_HEAD).astype(BF16)}
        q_p, c_p, kpe_p, kx_p, vx_p = _mla_prep(cols, lay, 0, rp, True, tabs, mprm)
        q_s, c_s, kpe_s = _mla_prep(cols, lay, rp, rs, False, tabs, mprm)
        o_p = _prompt_attn(q_p, kx_p, vx_p, nb, tp, nh)

        qa, qr = _sample_q(q_s, mprm, nh)
        qa = qa.reshape(db, ds * nh, kvl)
        qr = qr.reshape(db, ds, nh, 128)[..., :QK_ROPE].reshape(db, ds * nh, QK_ROPE)
        pad8 = lambda a_: jnp.concatenate([a_, jnp.zeros((db, 8 - ds, a_.shape[-1]), F32)], axis=1)
        rnew_t = jnp.pad(jnp.swapaxes(kpe_s.reshape(db, ds, QK_ROPE), 1, 2), ((0, 0), (0, 0), (0, 128 - ds)))
        olat = _sample_attn(page_table, mprm["w_uk"].T, qa, qr, pad8(c_s.reshape(db, ds, kvl)), rnew_t,
                            cache_kv_latent[l], jnp.swapaxes(cache_k_rope[l], 1, 2), nh, ds)
        o_s = _sample_o(olat.reshape(rs, nh * kvl), mprm["w_uv"], nh)
        yb = jnp.concatenate([o_p, o_s], axis=0)

        merged = _merge(ya, yb, cols, lay, w_branch_a[l].astype(BF16), w_branch_b[l].astype(BF16))
        wr = jnp.concatenate([w_router[l], jnp.zeros((d, 128 - n_experts), F32)], axis=1)
        wr_hi = wr.astype(BF16)
        wr_lo = (wr - wr_hi.astype(F32)).astype(BF16)
        br = jnp.concatenate([b_router[l], jnp.zeros((128 - n_experts,), F32)])[None]
        x1, hmoe, gate, idx = _out_router(merged, x, w_out[l].astype(BF16), ln2_g[l][None], wr_hi, wr_lo, br, n_experts)

        dff = w_down.shape[2]
        assert d // MOE_TN == dff // MOE_TF
        tok_sorted, pos, nused, work = _moe_routing(idx[:, :TOP_K], segments, r_all, n_experts, MOE_BM,
                                                    dff // MOE_TF)
        xs = _moe_gather(hmoe, tok_sorted, nused, MOE_BM, d // 128)
        act = _moe_up(xs, w_gu[l], b_gu[l][:, None, :], work, MOE_BM, MOE_TF)
        ys = _moe_down(act, w_down[l], b_down[l][:, None, :], work, MOE_BM, MOE_TN)
        x = _moe_combine(x1, ys, pos, gate)

        outs["lat_p"].append(c_p.reshape(nb, tp, kvl)[:, :t])
        outs["kpe_p"].append(kpe_p.reshape(nb, tp, QK_ROPE)[:, :t])
        outs["wkv_p"].append(wkv_p)
        last_p = np.arange(nb, dtype=np.int32) * tp + (t - 1)
        last_s = rp + np.arange(db, dtype=np.int32) * ds + (ds - 1)
        outs["sh_p"].append(_unpad_shift(cols[last_p], lay, d, n_w, n_a, n_g))
        outs["lat_s"].append(c_s.reshape(db, ds, kvl))
        outs["kpe_s"].append(kpe_s.reshape(db, ds, QK_ROPE))
        outs["wkv_s"].append(wkv_s)
        outs["sh_s"].append(_unpad_shift(cols[last_s], lay, d, n_w, n_a, n_g))

    y_prompt = jnp.stack([x[b_ * tp + n_meta:b_ * tp + t] for b_ in range(nb)])
    y_sample = x[rp:].reshape(db, ds, d)
    st = lambda k_: jnp.stack(outs[k_])
    return (y_prompt, y_sample, st("lat_p"), st("kpe_p"), st("wkv_p"), st("sh_p"),
            st("lat_s"), st("kpe_s"), st("wkv_s"), st("sh_s"))
```

```python
import functools

import numpy as np
import jax
import jax.numpy as jnp
from jax import lax
from jax.experimental import pallas as pl
from jax.experimental.pallas import tpu as pltpu

F32 = jnp.float32
BF16 = jnp.bfloat16

RWKV_HEAD = 64
QK_NOPE = 128
QK_ROPE = 64
QK_DIM = QK_NOPE + QK_ROPE
V_HEAD = 128
QK_PAD = 256
ROPE_THETA = 10000.0
RWKV_GN_EPS = 64e-5
NORM_EPS = 1e-6
NEG_INF = -1e30
TOP_K = 4
SWIGLU_LIMIT = 7.0
SWIGLU_ALPHA = 1.702

VMEM_LIMIT = 56 * 1024 * 1024

ATTN_TILE = 256
MOE_BM = 256
MOE_TF = 1024
MOE_TN = 1024


def _params(sem):
    return pltpu.CompilerParams(dimension_semantics=sem, vmem_limit_bytes=VMEM_LIMIT)


def _split(x):
    hi = x.astype(BF16)
    lo = (x - hi.astype(F32)).astype(BF16)
    return hi, lo


def _dot2(x, m):
    hi, lo = _split(x)
    return (jnp.dot(hi, m, preferred_element_type=F32) + jnp.dot(lo, m, preferred_element_type=F32))


def _sigmoid(x):
    return 1.0 / (1.0 + jnp.exp(-x))


def _seg_matrix(n, seg):
    e = (np.arange(n)[:, None] // seg == np.arange(n // seg)[None, :]).astype(np.float32)
    return jnp.asarray(e, BF16), jnp.asarray(e.T, BF16)


def _in_proj_kernel(x_ref, g_ref, w_ref, o_ref, xn_ref):
    @pl.when(pl.program_id(1) == 0)
    def _():
        x = x_ref[...]
        ms = jnp.mean(x * x, axis=-1, keepdims=True)
        xn_ref[...] = (x * lax.rsqrt(ms + NORM_EPS) * g_ref[...]).astype(BF16)

    o_ref[...] = jnp.dot(xn_ref[...], w_ref[...], preferred_element_type=F32)


def _in_proj(x, g, w, tm=1024, tn=512):
    r, d = x.shape
    nc = w.shape[1]
    return pl.pallas_call(
        _in_proj_kernel,
        grid=(r // tm, nc // tn),
        in_specs=[pl.BlockSpec((tm, d), lambda i, j: (i, 0)),
                  pl.BlockSpec((1, d), lambda i, j: (0, 0)),
                  pl.BlockSpec((d, tn), lambda i, j: (0, j))],
        out_specs=pl.BlockSpec((tm, tn), lambda i, j: (i, j)),
        out_shape=jax.ShapeDtypeStruct((r, nc), F32),
        scratch_shapes=[pltpu.VMEM((tm, d), BF16)],
        compiler_params=_params(("parallel", "arbitrary")),
        name="in_proj",
    )(x, g, w)


def _make_rwkv_prep_kernel(sample, tr, tiles_per_seq, t_real, ds):
    def kern(r_ref, k_ref, v_ref, l_ref, xr_ref, xk_ref, xv_ref, xl_ref,
             mur_ref, muk_ref, muv_ref, mul_ref, w0_ref, w2_ref, a0_ref, a2_ref, g2_ref,
             kk_ref, ka_ref, e_ref, et_ref,
             ro_ref, lwo_ref, ko_ref, vo_ref, kko_ref, bo_ref, go_ref):
        i = pl.program_id(0)

        def mixed(x_ref, extra_ref, mu_ref):
            x = x_ref[...]
            rolled = pltpu.roll(x, 1, axis=0)
            row = lax.broadcasted_iota(jnp.int32, x.shape, 0)
            if sample:
                prev = jnp.where(row % ds == 0, extra_ref[...], rolled)
            else:
                halo = jnp.where((i % tiles_per_seq) == 0, 0.0, extra_ref[7:8, :])
                prev = jnp.where(row == 0, halo, rolled)
            return x + mu_ref[...] * (prev - x)

        r = mixed(r_ref, xr_ref, mur_ref)
        k = mixed(k_ref, xk_ref, muk_ref)
        v = mixed(v_ref, xv_ref, muv_ref)
        lo = mixed(l_ref, xl_ref, mul_ref)
        w_in = lo[:, 0:128]
        a_in = lo[:, 128:256]
        g_in = lo[:, 256:512]
        z = w0_ref[...] + jnp.dot(jnp.tanh(w_in), w2_ref[...], preferred_element_type=F32)
        nz = -z
        softplus = jnp.maximum(nz, 0.0) + jnp.log(1.0 + jnp.exp(-jnp.abs(nz)))
        lw = -jnp.exp(-softplus - 0.5)
        a = _sigmoid(a0_ref[...] + jnp.dot(a_in, a2_ref[...], preferred_element_type=F32))
        g = jnp.dot(_sigmoid(g_in), g2_ref[...], preferred_element_type=F32)
        kk = k * kk_ref[...]
        ssq = _dot2(_dot2(kk * kk, e_ref[...]), et_ref[...])
        kk = kk * lax.rsqrt(jnp.maximum(ssq, 1e-24))
        k2 = k * (1.0 + (a - 1.0) * ka_ref[...])
        b = kk * a
        if not sample:
            row = lax.broadcasted_iota(jnp.int32, (tr, 1), 0) + (i % tiles_per_seq) * tr
            valid = row < t_real
            lw = jnp.where(valid, lw, 0.0)
            k2 = jnp.where(valid, k2, 0.0)
            kk = jnp.where(valid, kk, 0.0)
            b = jnp.where(valid, b, 0.0)
        ro_ref[...] = r
        lwo_ref[...] = lw
        ko_ref[...] = k2
        vo_ref[...] = v
        kko_ref[...] = kk
        bo_ref[...] = b
        go_ref[...] = g

    return kern


def _rwkv_prep(cols, lay, row0, nrows, sample, extras, prm, tiles_per_seq, t_real, ds, tr=128):
    d = lay["d"]
    rb0 = row0 // tr
    cb = lay["cb"]
    e64, et64 = _seg_matrix(d, RWKV_HEAD)

    def colspec(width, blk):
        return pl.BlockSpec((tr, width), lambda i: (rb0 + i, blk))

    in_specs = [colspec(d, cb["r"]), colspec(d, cb["k"]), colspec(d, cb["v"]), colspec(512, cb["lora"])]
    args = [cols, cols, cols, cols]
    if sample:
        for a_, w_ in zip(extras, (d, d, d, 512)):
            in_specs.append(pl.BlockSpec((tr, w_), lambda i: (i, 0)))
            args.append(a_)
    else:
        def halospec(width, blk):
            return pl.BlockSpec((8, width), lambda i: (jnp.maximum((rb0 + i) * (tr // 8) - 1, 0), blk))
        in_specs += [halospec(d, cb["r"]), halospec(d, cb["k"]), halospec(d, cb["v"]), halospec(512, cb["lora"])]
        args += [cols, cols, cols, cols]
    for name in ("mu_r", "mu_k", "mu_v", "mu_l", "w0", "w2", "a0", "a2", "g2", "k_k", "k_a"):
        a_ = prm[name]
        in_specs.append(pl.BlockSpec(a_.shape, lambda i: (0, 0)))
        args.append(a_)
    in_specs += [pl.BlockSpec(e64.shape, lambda i: (0, 0)), pl.BlockSpec(et64.shape, lambda i: (0, 0))]
    args += [e64, et64]
    out = jax.ShapeDtypeStruct((nrows, d), F32)
    return pl.pallas_call(
        _make_rwkv_prep_kernel(sample, tr, tiles_per_seq, t_real, ds),
        grid=(nrows // tr,),
        in_specs=in_specs,
        out_specs=[pl.BlockSpec((tr, d), lambda i: (i, 0))] * 7,
        out_shape=[out] * 7,
        compiler_params=_params(("parallel",)),
        name="rwkv_prep_sample" if sample else "rwkv_prep_prompt",
    )(*args)


_NT = (((1,), (1,)), ((), ()))
_TN = (((0,), (0,)), ((), ()))
PAIR = 2 * RWKV_HEAD


def _wkv_masks(c):
    rt = lax.broadcasted_iota(jnp.int32, (2 * c, 2 * c), 0)
    ct = lax.broadcasted_iota(jnp.int32, (2 * c, 2 * c), 1)
    same = (rt // c) == (ct // c)
    strict = same & ((ct % c) < (rt % c))
    incl = same & ((ct % c) <= (rt % c))
    lane = lax.broadcasted_iota(jnp.int32, (c, PAIR), 1)
    return strict, incl, lane < RWKV_HEAD


def _wkv_pair_chunks(items, masks, c):
    strict, incl, m0 = masks
    n = range(len(items))
    c2 = 2 * c
    dot = functools.partial(jnp.dot, preferred_element_type=F32)
    b16 = lambda x: x.astype(BF16)

    def ext(x):
        return jnp.concatenate([jnp.where(m0, x, 0.0), jnp.where(m0, 0.0, x)], axis=0).astype(BF16)

    qr = [jnp.concatenate([ext(it[2]), ext(it[1])], axis=0) for it in items]
    nbk = [jnp.concatenate([ext(-it[3]), ext(it[4])], axis=0) for it in items]
    vx = [ext(it[5]) for it in items]
    a = [lax.dot_general(qr[i], nbk[i], _NT, preferred_element_type=F32) for i in n]
    qg = [lax.dot_general(qr[i], b16(items[i][0]), _NT, preferred_element_type=F32) for i in n]
    lb = [jnp.where(strict, -a[i][0:c2, 0:c2], 0.0) for i in n]
    lk = [b16(jnp.where(strict, a[i][0:c2, c2:2 * c2], 0.0)) for i in n]
    mbk = [b16(jnp.concatenate([jnp.where(incl, a[i][c2:2 * c2, 0:c2], 0.0),
                                jnp.where(incl, a[i][c2:2 * c2, c2:2 * c2], 0.0)], axis=1)) for i in n]
    lp = [b16(x) for x in lb]
    u = [qg[i][0:c2] + dot(lk[i], vx[i]) for i in n]
    sign = -1.0
    span = 1
    while span < c:
        nxt = [dot(lp[i], lp[i]) for i in n] if 2 * span < c else None
        u = [u[i] + sign * dot(lp[i], b16(u[i])) for i in n]
        if nxt is not None:
            lp = [b16(x) for x in nxt]
        sign = 1.0
        span *= 2
    ux = [jnp.concatenate([b16(u[i]), vx[i]], axis=0) for i in n]
    y = [qg[i][c2:2 * c2] + dot(mbk[i], ux[i]) for i in n]
    ds_ = [lax.dot_general(ux[i], nbk[i], _TN, preferred_element_type=F32) for i in n]
    return [(y[i][0:c] + y[i][c:c2], (items[i][0] + ds_[i]) * items[i][6]) for i in n]


def _pair_state(s0, s1):
    z = jnp.zeros((RWKV_HEAD, RWKV_HEAD), F32)
    return jnp.concatenate([jnp.concatenate([s0, z], axis=1), jnp.concatenate([z, s1], axis=1)], axis=0)


def _make_wkv_prompt_kernel(c, nchunk, g, t_real):
    rb = c * nchunk

    def kern(r_ref, lw_ref, k_ref, v_ref, kk_ref, b_ref, y_ref, so_ref, s_ref):
        i = pl.program_id(2)

        @pl.when(i == 0)
        def _():
            s_ref[...] = jnp.zeros(s_ref.shape, F32)

        masks = _wkv_masks(c)
        tril = (lax.broadcasted_iota(jnp.int32, (c, c), 0) >= lax.broadcasted_iota(jnp.int32, (c, c), 1)).astype(BF16)
        nreal = jnp.clip((t_real - i * rb + c - 1) // c, 0, nchunk)

        @pl.when(nreal < nchunk)
        def _():
            y_ref[...] = jnp.zeros(y_ref.shape, F32)

        def chunk(ci, carry):
            rows = pl.ds(pl.multiple_of(ci * c, c), c)
            lw = lw_ref[rows, :]
            hi, lo = _split(lw)
            cs = jnp.dot(tril, hi, preferred_element_type=F32) + jnp.dot(tril, lo, preferred_element_type=F32)
            e_pos = jnp.exp(cs)
            e_neg = jnp.exp(-cs)
            rt = r_ref[rows, :] * e_pos
            qh = kk_ref[rows, :] * jnp.exp(cs - lw)
            bt = b_ref[rows, :] * e_neg
            kt = k_ref[rows, :] * e_neg
            vv = v_ref[rows, :]
            items = []
            for p in range(g):
                sl = slice(p * PAIR, (p + 1) * PAIR)
                items.append((s_ref[p], rt[:, sl], qh[:, sl], bt[:, sl], kt[:, sl], vv[:, sl], e_pos[c - 1:c, sl]))
            outs = _wkv_pair_chunks(items, masks, c)
            for p in range(g):
                s_ref[p] = outs[p][1]
                y_ref[rows, p * PAIR:(p + 1) * PAIR] = outs[p][0]
            return carry

        lax.fori_loop(0, nreal, chunk, 0)

        @pl.when(i == pl.num_programs(2) - 1)
        def _():
            for p in range(g):
                s = s_ref[p]
                so_ref[0, 2 * p] = s[0:RWKV_HEAD, 0:RWKV_HEAD]
                so_ref[0, 2 * p + 1] = s[RWKV_HEAD:PAIR, RWKV_HEAD:PAIR]

    return kern


def _wkv_prompt(seqs, nb, tp, t_real, c=64, nchunk=4, g=8):
    d = seqs[0].shape[1]
    nh = d // RWKV_HEAD
    rb = c * nchunk
    nrb = tp // rb
    spec = pl.BlockSpec((rb, g * PAIR), lambda n, q, i: (n * nrb + i, q))
    return pl.pallas_call(
        _make_wkv_prompt_kernel(c, nchunk, g, t_real),
        grid=(nb, nh // (2 * g), nrb),
        in_specs=[spec] * 6,
        out_specs=[spec, pl.BlockSpec((1, 2 * g, RWKV_HEAD, RWKV_HEAD), lambda n, q, i: (n, q, 0, 0))],
        out_shape=[jax.ShapeDtypeStruct((nb * tp, d), F32),
                   jax.ShapeDtypeStruct((nb, nh, RWKV_HEAD, RWKV_HEAD), F32)],
        scratch_shapes=[pltpu.VMEM((g, PAIR, PAIR), F32)],
        compiler_params=_params(("parallel", "parallel", "arbitrary")),
        name="wkv_prompt",
    )(*seqs)


def _make_wkv_sample_kernel(c, ns, g):
    def kern(r_ref, lw_ref, k_ref, v_ref, kk_ref, b_ref, s0_ref, y_ref, so_ref):
        masks = _wkv_masks(c)
        row = lax.broadcasted_iota(jnp.int32, (c, g * PAIR), 0)
        items = []
        for n in range(ns):
            rows = slice(n * c, (n + 1) * c)
            lw = lw_ref[rows, :]
            cs = jnp.zeros(lw.shape, F32)
            for j in range(c):
                cs = cs + jnp.where(row >= j, lw[j:j + 1, :], 0.0)
            e_pos = jnp.exp(cs)
            e_neg = jnp.exp(-cs)
            rt = r_ref[rows, :] * e_pos
            qh = kk_ref[rows, :] * jnp.exp(cs - lw)
            bt = b_ref[rows, :] * e_neg
            kt = k_ref[rows, :] * e_neg
            vv = v_ref[rows, :]
            for p in range(g):
                sl = slice(p * PAIR, (p + 1) * PAIR)
                s = _pair_state(s0_ref[n, 2 * p], s0_ref[n, 2 * p + 1])
                items.append((s, rt[:, sl], qh[:, sl], bt[:, sl], kt[:, sl], vv[:, sl], e_pos[c - 1:c, sl]))
        outs = _wkv_pair_chunks(items, masks, c)
        for n in range(ns):
            for p in range(g):
                y, s = outs[n * g + p]
                y_ref[n * c:(n + 1) * c, p * PAIR:(p + 1) * PAIR] = y
                so_ref[n, 2 * p] = s[0:RWKV_HEAD, 0:RWKV_HEAD]
                so_ref[n, 2 * p + 1] = s[RWKV_HEAD:PAIR, RWKV_HEAD:PAIR]

    return kern


def _wkv_sample(seqs, s0, c, ns=4, g=4):
    d = seqs[0].shape[1]
    n = s0.shape[0]
    nh = d // RWKV_HEAD
    spec = pl.BlockSpec((ns * c, g * PAIR), lambda i, q: (i, q))
    st = pl.BlockSpec((ns, 2 * g, RWKV_HEAD, RWKV_HEAD), lambda i, q: (i, q, 0, 0))
    return pl.pallas_call(
        _make_wkv_sample_kernel(c, ns, g),
        grid=(n // ns, nh // (2 * g)),
        in_specs=[spec] * 6 + [st],
        out_specs=[spec, st],
        out_shape=[jax.ShapeDtypeStruct((n * c, d), F32),
                   jax.ShapeDtypeStruct((n, nh, RWKV_HEAD, RWKV_HEAD), F32)],
        compiler_params=_params(("parallel", "parallel")),
        name="wkv_sample",
    )(*seqs, s0)


def _rwkv_post_kernel(y_ref, r_ref, k_ref, v_ref, g_ref, lng_ref, lnb_ref, rk_ref, e_ref, et_ref, o_ref):
    e = e_ref[...]
    et = et_ref[...]
    inv = 1.0 / RWKV_HEAD
    y = y_ref[...]
    mu = _dot2(_dot2(y, e), et) * inv
    dlt = y - mu
    var = _dot2(_dot2(dlt * dlt, e), et) * inv
    yn = dlt * lax.rsqrt(var + RWKV_GN_EPS) * lng_ref[...] + lnb_ref[...]
    v = v_ref[...]
    bonus = _dot2(_dot2(r_ref[...] * k_ref[...] * rk_ref[...], e), et) * v
    o_ref[...] = ((yn + bonus) * g_ref[...]).astype(BF16)


def _rwkv_post(y, r, k2, v, g, prm, tr=128):
    nrows, d = y.shape
    e64, et64 = _seg_matrix(d, RWKV_HEAD)
    row = pl.BlockSpec((tr, d), lambda i: (i, 0))
    vec = pl.BlockSpec((1, d), lambda i: (0, 0))
    return pl.pallas_call(
        _rwkv_post_kernel,
        grid=(nrows // tr,),
        in_specs=[row] * 5 + [vec] * 3 + [pl.BlockSpec(e64.shape, lambda i: (0, 0)),
                                          pl.BlockSpec(et64.shape, lambda i: (0, 0))],
        out_specs=row,
        out_shape=jax.ShapeDtypeStruct((nrows, d), BF16),
        compiler_params=_params(("parallel",)),
        name="rwkv_post",
    )(y, r, k2, v, g, prm["ln_g"], prm["ln_b"], prm["r_k"], e64, et64)


def _make_mla_prep_kernel(prompt, nh):
    inv_dim = 1.0 / QK_DIM

    def kern(*refs):
        (qn_ref, q1_ref, q2_ref, c_ref, kp_ref, cos_ref, sin_ref, c128_ref, s128_ref,
         gqn_ref, gqr_ref, gkv_ref, e128_ref, et128_ref, e32_ref, et32_ref, p1_ref, p2_ref) = refs[:18]
        if prompt:
            (gkn_ref, gkr_ref, wuk_ref, wuv_ref, qo_ref, co_ref, ko_ref, kout_ref, vout_ref) = refs[18:]
        else:
            (qo_ref, co_ref, ko_ref) = refs[18:]
        scale = QK_DIM ** -0.5
        qn = qn_ref[...]
        cos = cos_ref[...]
        sin = sin_ref[...]
        q1 = q1_ref[...]
        q2 = q2_ref[...]
        r1 = q1 * cos - q2 * sin
        r2 = q2 * cos + q1 * sin
        ssq = _dot2(qn * qn, e128_ref[...]) + _dot2(r1 * r1, e32_ref[...]) + _dot2(r2 * r2, e32_ref[...])
        rf = lax.rsqrt(ssq * inv_dim + NORM_EPS) * scale
        qn = qn * _dot2(rf, et128_ref[...]) * gqn_ref[...]
        rf32 = _dot2(rf, et32_ref[...]) * gqr_ref[...]
        r1 = (r1 * rf32).astype(BF16)
        r2 = (r2 * rf32).astype(BF16)
        qr = (jnp.dot(r1, p1_ref[...], preferred_element_type=F32)
              + jnp.dot(r2, p2_ref[...], preferred_element_type=F32)).astype(BF16)
        qnb = qn.astype(BF16)
        for h in range(nh):
            qo_ref[:, h * QK_PAD:h * QK_PAD + QK_NOPE] = qnb[:, h * QK_NOPE:(h + 1) * QK_NOPE]
            qo_ref[:, h * QK_PAD + QK_NOPE:(h + 1) * QK_PAD] = qr[:, h * 128:(h + 1) * 128]
        c = c_ref[...]
        c = c * lax.rsqrt(jnp.mean(c * c, axis=-1, keepdims=True) + NORM_EPS) * gkv_ref[...]
        co_ref[...] = c
        kp = kp_ref[...]
        lane = lax.broadcasted_iota(jnp.int32, kp.shape, 1)
        swap = jnp.where(lane < QK_ROPE // 2, pltpu.roll(kp, 128 - QK_ROPE // 2, axis=1),
                         pltpu.roll(kp, QK_ROPE // 2, axis=1))
        kr = kp * c128_ref[...] + swap * s128_ref[...]
        ko_ref[...] = kr[:, :QK_ROPE]
        if prompt:
            cb = c.astype(BF16)
            kn = jnp.dot(cb, wuk_ref[...], preferred_element_type=F32)
            ssqk = _dot2(kn * kn, e128_ref[...]) + jnp.sum(kr * kr, axis=-1, keepdims=True)
            rk = _dot2(lax.rsqrt(ssqk * inv_dim + NORM_EPS), et128_ref[...])
            knb = (kn * rk * gkn_ref[...]).astype(BF16)
            krg = kr * gkr_ref[...]
            krb = (jnp.concatenate([krg] * nh, axis=1) * rk).astype(BF16)
            for h in range(nh):
                kout_ref[:, h * QK_PAD:h * QK_PAD + QK_NOPE] = knb[:, h * QK_NOPE:(h + 1) * QK_NOPE]
                kout_ref[:, h * QK_PAD + QK_NOPE:(h + 1) * QK_PAD] = krb[:, h * 128:(h + 1) * 128]
            vb = jnp.dot(cb, wuv_ref[...], preferred_element_type=F32).astype(BF16)
            ones = jnp.ones((vb.shape[0], V_HEAD), BF16)
            for h in range(nh):
                vout_ref[:, 2 * h * V_HEAD:(2 * h + 1) * V_HEAD] = vb[:, h * V_HEAD:(h + 1) * V_HEAD]
                vout_ref[:, (2 * h + 1) * V_HEAD:(2 * h + 2) * V_HEAD] = ones

    return kern


def _mla_prep(cols, lay, row0, nrows, prompt, tabs, prm, tr=256):
    nh = lay["mla_heads"]
    rb0 = row0 // tr
    cb = lay["cb"]
    dq = nh * QK_NOPE
    dr = nh * (QK_ROPE // 2)
    kvl = prm["kv_g"].shape[1]

    def colspec(width, blk):
        return pl.BlockSpec((tr, width), lambda i: (rb0 + i, blk))

    def tabspec(width):
        return pl.BlockSpec((tr, width), lambda i: (rb0 + i, 0))

    def full(a_):
        return pl.BlockSpec(a_.shape, lambda i: (0,) * a_.ndim)

    e128, et128 = _seg_matrix(dq, QK_NOPE)
    e32, et32 = _seg_matrix(dr, QK_ROPE // 2)
    idx = np.arange(dr)
    p1 = np.zeros((dr, nh * 128), np.float32)
    p2 = np.zeros((dr, nh * 128), np.float32)
    p1[idx, (idx // 32) * 128 + idx % 32] = 1.0
    p2[idx, (idx // 32) * 128 + 32 + idx % 32] = 1.0
    p1 = jnp.asarray(p1, BF16)
    p2 = jnp.asarray(p2, BF16)

    args = [cols, cols, cols, cols, cols, tabs["cos"], tabs["sin"], tabs["c128"], tabs["s128"],
            prm["gq_n"], prm["gq_r"], prm["kv_g"], e128, et128, e32, et32, p1, p2]
    in_specs = [colspec(dq, cb["q_nope"]), colspec(dr, cb["q_r1"]), colspec(dr, cb["q_r2"]),
                colspec(kvl, cb["c"]), colspec(128, cb["kpe"]),
                tabspec(dr), tabspec(dr), tabspec(128), tabspec(128)]
    in_specs += [full(a_) for a_ in args[9:]]
    row = lambda w_: pl.BlockSpec((tr, w_), lambda i: (i, 0))
    out_specs = [row(nh * QK_PAD), row(kvl), row(QK_ROPE)]
    out_shape = [jax.ShapeDtypeStruct((nrows, nh * QK_PAD), BF16),
                 jax.ShapeDtypeStruct((nrows, kvl), F32),
                 jax.ShapeDtypeStruct((nrows, QK_ROPE), F32)]
    if prompt:
        extra = [prm["gk_n"], prm["gk_r128"], prm["w_uk"], prm["w_uv"]]
        args += extra
        in_specs += [full(a_) for a_ in extra]
        out_specs += [row(nh * QK_PAD), row(nh * 2 * V_HEAD)]
        out_shape += [jax.ShapeDtypeStruct((nrows, nh * QK_PAD), BF16),
                      jax.ShapeDtypeStruct((nrows, nh * 2 * V_HEAD), BF16)]
    return pl.pallas_call(
        _make_mla_prep_kernel(prompt, nh),
        grid=(nrows // tr,),
        in_specs=in_specs,
        out_specs=out_specs,
        out_shape=out_shape,
        compiler_params=_params(("parallel",)),
        name="mla_prep_prompt" if prompt else "mla_prep_sample",
    )(*args)


def _make_prompt_attn_kernel(tile, hb):
    def kern(q_ref, k_ref, v_ref, o_ref, m_ref, acc_ref):
        qi = pl.program_id(2)
        m_ref[...] = jnp.full(m_ref.shape, NEG_INF, F32)
        acc_ref[...] = jnp.zeros(acc_ref.shape, F32)
        heads = range(hb)
        qs = [q_ref[:, h * QK_PAD:(h + 1) * QK_PAD] for h in heads]
        causal = (lax.broadcasted_iota(jnp.int32, (tile, tile), 1) <= lax.broadcasted_iota(jnp.int32, (tile, tile), 0))

        def kv_tile(j, diagonal):
            rows = pl.ds(pl.multiple_of(j * tile, tile), tile)
            ss = [lax.dot_general(qs[h], k_ref[rows, h * QK_PAD:(h + 1) * QK_PAD], _NT,
                                  preferred_element_type=F32) for h in heads]
            if diagonal:
                ss = [jnp.where(causal, s, NEG_INF) for s in ss]
            ms = [m_ref[h] for h in heads]
            mn = [jnp.maximum(ms[h], jnp.max(ss[h], axis=1, keepdims=True)) for h in heads]
            ps = [jnp.exp(ss[h] - jnp.concatenate([mn[h]] * (tile // 128), axis=1)) for h in heads]
            pv = [jnp.dot(ps[h].astype(BF16), v_ref[rows, h * 2 * V_HEAD:(h + 1) * 2 * V_HEAD],
                          preferred_element_type=F32) for h in heads]
            for h in heads:
                alpha = jnp.exp(ms[h] - mn[h])
                acc_ref[h] = jnp.concatenate([alpha, alpha], axis=1) * acc_ref[h] + pv[h]
                m_ref[h] = mn[h]

        def body(j, carry):
            kv_tile(j, False)
            return carry

        lax.fori_loop(0, qi, body, 0)
        kv_tile(qi, True)
        for h in heads:
            acc = acc_ref[h]
            o_ref[:, h * V_HEAD:(h + 1) * V_HEAD] = (acc[:, :V_HEAD] / acc[:, V_HEAD:]).astype(BF16)

    return kern


def _prompt_attn(q, k, v, nb, tp, nh, tile=ATTN_TILE, hb=4):
    nq = tp // tile
    return pl.pallas_call(
        _make_prompt_attn_kernel(tile, hb),
        grid=(nb, nh // hb, nq),
        in_specs=[pl.BlockSpec((tile, hb * QK_PAD), lambda b, h, i: (b * nq + i, h)),
                  pl.BlockSpec((tp, hb * QK_PAD), lambda b, h, i: (b, h)),
                  pl.BlockSpec((tp, hb * 2 * V_HEAD), lambda b, h, i: (b, h))],
        out_specs=pl.BlockSpec((tile, hb * V_HEAD), lambda b, h, i: (b * nq + i, h)),
        out_shape=jax.ShapeDtypeStruct((nb * tp, nh * V_HEAD), BF16),
        scratch_shapes=[pltpu.VMEM((hb, tile, 128), F32), pltpu.VMEM((hb, tile, 2 * V_HEAD), F32)],
        compiler_params=_params(("parallel", "parallel", "arbitrary")),
        name="prompt_attn",
    )(q, k, v)


def _sample_q_kernel(q_ref, gkn_ref, gkr_ref, wuk_ref, qa_ref, qr_ref):
    q = q_ref[...].astype(F32)
    qn = (q[:, :QK_NOPE] * gkn_ref[...]).astype(BF16)
    qa_ref[...] = lax.dot_general(qn, wuk_ref[...], (((1,), (1,)), ((), ())),
                                  preferred_element_type=F32).astype(BF16)
    qr_ref[...] = (q[:, QK_NOPE:] * gkr_ref[...]).astype(BF16)


def _sample_q(q, prm, nh):
    rs = q.shape[0]
    kvl = prm["w_uk"].shape[0]
    return pl.pallas_call(
        _sample_q_kernel,
        grid=(nh,),
        in_specs=[pl.BlockSpec((rs, QK_PAD), lambda h: (0, h)),
                  pl.BlockSpec((1, QK_NOPE), lambda h: (0, 0)),
                  pl.BlockSpec((1, 128), lambda h: (0, 0)),
                  pl.BlockSpec((kvl, QK_NOPE), lambda h: (0, h))],
        out_specs=[pl.BlockSpec((rs, kvl), lambda h: (0, h)),
                   pl.BlockSpec((rs, 128), lambda h: (0, h))],
        out_shape=[jax.ShapeDtypeStruct((rs, nh * kvl), BF16),
                   jax.ShapeDtypeStruct((rs, nh * 128), BF16)],
        compiler_params=_params(("parallel",)),
        name="sample_q",
    )(q, prm["gk_n1"], prm["gk_r128"], prm["w_uk"])


SA_SLOTS = 16
SA_UNROLL = 4


def _make_sample_attn_kernel(nh, ds, kvl, page, n_pages, chunk_pages):
    nq = nh * ds
    nw = nh * QK_NOPE
    ct = chunk_pages * page
    nch = n_pages // chunk_pages
    inv_dim = 1.0 / QK_DIM

    new = SA_SLOTS
    pre = SA_SLOTS - SA_UNROLL
    assert nch % SA_SLOTS == 0 and nch % SA_UNROLL == 0 and SA_UNROLL % 2 == 0 and pre > SA_UNROLL

    def kern(pt_ref, wt_ref, qa_ref, qr_ref, cnew_ref, rnew_ref, cc_ref, cr_ref, o_ref,
             l_ref, cbuf, rbuf, kt0, kt1, a20, a21, sem):
        n = pl.program_id(0)
        nseq = pl.num_programs(0)

        def copies(seq, j, slot):
            out = []
            for p in range(chunk_pages):
                pg = pt_ref[seq * n_pages + j * chunk_pages + p]
                out.append(pltpu.make_async_copy(cc_ref.at[pg], cbuf.at[slot, pl.ds(p * page, page)],
                                                 sem.at[slot, 2 * p]))
                out.append(pltpu.make_async_copy(cr_ref.at[pg], rbuf.at[slot, :, pl.ds(p * page, page)],
                                                 sem.at[slot, 2 * p + 1]))
            return out

        def request(t):
            seq2 = n + t // nch
            ch2 = t % nch

            @pl.when(seq2 < nseq)
            def _():
                for cp in copies(seq2, ch2, ch2 % SA_SLOTS):
                    cp.start()

        def arrive(j):
            @pl.when(j < nch)
            def _():
                for cp in copies(n, j, j % SA_SLOTS):
                    cp.wait()

        @pl.when(n == 0)
        def _():
            l_ref[0:nw, :] = wt_ref[...]
            for j in range(pre):
                for cp in copies(0, j, j):
                    cp.start()

        l_ref[nw:nw + nq, :] = qa_ref[0]
        cbuf[new] = jnp.zeros((ct, kvl), F32)
        rbuf[new] = jnp.zeros((QK_ROPE, ct), F32)
        cbuf[new, 0:8, :] = cnew_ref[0]
        rbuf[new, :, 0:128] = rnew_ref[0]

        def scores(j, kt_ref, a2_ref):
            slot = jnp.where(j < nch, j % SA_SLOTS, new)
            cb = cbuf[slot].astype(BF16)
            rb = rbuf[slot].astype(BF16)
            kt_ref[...] = lax.dot_general(l_ref[...], cb, _NT, preferred_element_type=F32)
            a2_ref[...] = jnp.dot(qr_ref[0], rb, preferred_element_type=F32)

        def softmax_pv(slot, kt_ref, a2_ref, carry, causal):
            m, l, acc = carry
            kn = kt_ref[0:nw, :].reshape(nh, QK_NOPE, ct)
            ssq = jnp.sum(kn * kn, axis=1)
            kpt = rbuf[slot]
            ssq = ssq + jnp.sum(kpt * kpt, axis=0, keepdims=True)
            rf = lax.rsqrt(ssq * inv_dim + NORM_EPS)
            s = (kt_ref[nw:nw + nq, :] + a2_ref[...]) * jnp.concatenate([rf] * ds, axis=0)
            if causal:
                qidx = lax.broadcasted_iota(jnp.int32, (nq, ct), 0) // nh
                kidx = lax.broadcasted_iota(jnp.int32, (nq, ct), 1)
                s = jnp.where(kidx <= qidx, s, NEG_INF)
            m_new = jnp.maximum(m, jnp.max(s, axis=1, keepdims=True))
            alpha = jnp.exp(m - m_new)
            p = jnp.exp(s - m_new)
            l = alpha * l + jnp.sum(p, axis=1, keepdims=True)
            acc = alpha * acc + jnp.dot(p.astype(BF16), cbuf[slot].astype(BF16), preferred_element_type=F32)
            return m_new, l, acc

        arrive(0)
        scores(0, kt0, a20)

        def body(i, carry):
            j = SA_UNROLL * i
            for k_ in range(SA_UNROLL):
                request(j + pre + k_)
            for k_ in range(1, SA_UNROLL + 1):
                arrive(j + k_)
            bufs = ((kt0, a20), (kt1, a21))
            for k_ in range(SA_UNROLL):
                scores(j + k_ + 1, *bufs[(k_ + 1) % 2])
                carry = softmax_pv((j + k_) % SA_SLOTS, *bufs[k_ % 2], carry, False)
            return carry

        init = (jnp.full((nq, 1), NEG_INF, F32), jnp.zeros((nq, 1), F32), jnp.zeros((nq, kvl), F32))
        carry = lax.fori_loop(0, nch // SA_UNROLL, body, init)
        m, l, acc = softmax_pv(new, kt0, a20, carry, True)
        o_ref[0] = acc / l

    return kern


def _sample_attn(page_table, wt, qa, qr, cnew, rnew_t, cache_c, cache_rt, nh, ds, chunk_pages=2):
    db, n_pages = page_table.shape
    _, page, kvl = cache_c.shape
    nq = nh * ds
    nw = nh * QK_NOPE
    ct = chunk_pages * page
    assert page == 128
    grid_spec = pltpu.PrefetchScalarGridSpec(
        num_scalar_prefetch=1,
        grid=(db,),
        in_specs=[pl.BlockSpec((nw, kvl), lambda n, pt: (0, 0)),
                  pl.BlockSpec((1, nq, kvl), lambda n, pt: (n, 0, 0)),
                  pl.BlockSpec((1, nq, QK_ROPE), lambda n, pt: (n, 0, 0)),
                  pl.BlockSpec((1, 8, kvl), lambda n, pt: (n, 0, 0)),
                  pl.BlockSpec((1, QK_ROPE, 128), lambda n, pt: (n, 0, 0)),
                  pl.BlockSpec(memory_space=pl.ANY),
                  pl.BlockSpec(memory_space=pl.ANY)],
        out_specs=pl.BlockSpec((1, nq, kvl), lambda n, pt: (n, 0, 0)),
        scratch_shapes=[pltpu.VMEM((nw + nq, kvl), BF16),
                        pltpu.VMEM((SA_SLOTS + 1, ct, kvl), F32),
                        pltpu.VMEM((SA_SLOTS + 1, QK_ROPE, ct), F32),
                        pltpu.VMEM((nw + nq, ct), F32), pltpu.VMEM((nw + nq, ct), F32),
                        pltpu.VMEM((nq, ct), F32), pltpu.VMEM((nq, ct), F32),
                        pltpu.SemaphoreType.DMA((SA_SLOTS, 2 * chunk_pages))],
    )
    return pl.pallas_call(
        _make_sample_attn_kernel(nh, ds, kvl, page, n_pages, chunk_pages),
        grid_spec=grid_spec,
        out_shape=jax.ShapeDtypeStruct((db, nq, kvl), F32),
        compiler_params=_params(("arbitrary",)),
        name="sample_attn",
    )(page_table.reshape(-1), wt, qa, qr, cnew, rnew_t, cache_c, cache_rt)


def _sample_o_kernel(ol_ref, wuv_ref, o_ref):
    o_ref[...] = jnp.dot(ol_ref[...].astype(BF16), wuv_ref[...], preferred_element_type=F32).astype(BF16)


def _sample_o(olat, w_uv, nh):
    rs = olat.shape[0]
    kvl = w_uv.shape[0]
    return pl.pallas_call(
        _sample_o_kernel,
        grid=(nh,),
        in_specs=[pl.BlockSpec((rs, kvl), lambda h: (0, h)),
                  pl.BlockSpec((kvl, V_HEAD), lambda h: (0, h))],
        out_specs=pl.BlockSpec((rs, V_HEAD), lambda h: (0, h)),
        out_shape=jax.ShapeDtypeStruct((rs, nh * V_HEAD), BF16),
        compiler_params=_params(("parallel",)),
        name="sample_o",
    )(olat, w_uv)


def _merge_kernel(ya_ref, yb_ref, ga_ref, gb_ref, wa_ref, wb_ref, o_ref):
    pa = jnp.dot(ya_ref[...], wa_ref[...], preferred_element_type=F32)
    pb = jnp.dot(yb_ref[...], wb_ref[...], preferred_element_type=F32)
    o_ref[...] = (_sigmoid(ga_ref[...]) * pa + _sigmoid(gb_ref[...]) * pb).astype(BF16)


def _merge(ya, yb, cols, lay, wa, wb, tm=512, tn=512):
    r, d = ya.shape
    nj = d // tn
    ga0 = lay["cb"]["gate_a"] * (d // tn)
    gb0 = lay["cb"]["gate_b"] * (d // tn)
    return pl.pallas_call(
        _merge_kernel,
        grid=(nj, r // tm),
        in_specs=[pl.BlockSpec((tm, d), lambda j, i: (i, 0)),
                  pl.BlockSpec((tm, d), lambda j, i: (i, 0)),
                  pl.BlockSpec((tm, tn), lambda j, i: (i, ga0 + j)),
                  pl.BlockSpec((tm, tn), lambda j, i: (i, gb0 + j)),
                  pl.BlockSpec((d, tn), lambda j, i: (0, j)),
                  pl.BlockSpec((d, tn), lambda j, i: (0, j))],
        out_specs=pl.BlockSpec((tm, tn), lambda j, i: (i, j)),
        out_shape=jax.ShapeDtypeStruct((r, d), BF16),
        compiler_params=_params(("parallel", "parallel")),
        name="branch_merge",
    )(ya, yb, cols, cols, wa, wb)


def _make_out_router_kernel(n_experts):
    def kern(m_ref, x_ref, wo_ref, g2_ref, wrh_ref, wrl_ref, br_ref, x1_ref, h_ref, gate_ref, idx_ref):
        x1 = x_ref[...] + jnp.dot(m_ref[...], wo_ref[...], preferred_element_type=F32)
        x1_ref[...] = x1
        h = x1 * lax.rsqrt(jnp.mean(x1 * x1, axis=-1, keepdims=True) + NORM_EPS) * g2_ref[...]
        for c_ in range(h.shape[1] // 128):
            h_ref[pl.ds(c_, h.shape[0], stride=h.shape[1] // 128), :] = h[:, c_ * 128:(c_ + 1) * 128]
        hh, hl = _split(h)
        logits = (jnp.dot(hh, wrh_ref[...], preferred_element_type=F32)
                  + jnp.dot(hh, wrl_ref[...], preferred_element_type=F32)
                  + jnp.dot(hl, wrh_ref[...], preferred_element_type=F32)) + br_ref[...]
        lane = lax.broadcasted_iota(jnp.int32, logits.shape, 1)
        work = jnp.where(lane < n_experts, logits, -jnp.inf)
        vals, idxs = [], []
        for _ in range(TOP_K):
            mx = jnp.max(work, axis=1, keepdims=True)
            ix = jnp.min(jnp.where(work == mx, lane, 128), axis=1, keepdims=True)
            vals.append(mx)
            idxs.append(ix)
            work = jnp.where(lane == ix, -jnp.inf, work)
        ex = [jnp.exp(v_ - vals[0]) for v_ in vals]
        den = ex[0] + ex[1] + ex[2] + ex[3]
        gate = jnp.zeros(logits.shape, F32)
        idx = jnp.zeros(logits.shape, jnp.int32)
        for k_ in range(TOP_K):
            gate = jnp.where(lane == k_, ex[k_] / den, gate)
            idx = jnp.where(lane == k_, idxs[k_], idx)
        gate_ref[...] = gate
        idx_ref[...] = idx

    return kern


def _out_router(merged, x, wo, ln2, wr_hi, wr_lo, br, n_experts, tm=256):
    r, d = x.shape
    row = pl.BlockSpec((tm, d), lambda i: (i, 0))
    small = pl.BlockSpec((tm, 128), lambda i: (i, 0))
    full = lambda a_: pl.BlockSpec(a_.shape, lambda i: (0, 0))
    return pl.pallas_call(
        _make_out_router_kernel(n_experts),
        grid=(r // tm,),
        in_specs=[row, row, full(wo), full(ln2), full(wr_hi), full(wr_lo), full(br)],
        out_specs=[row, pl.BlockSpec((tm * (d // 128), 128), lambda i: (i, 0)), small, small],
        out_shape=[jax.ShapeDtypeStruct((r, d), F32), jax.ShapeDtypeStruct((r * (d // 128), 128), F32),
                   jax.ShapeDtypeStruct((r, 128), F32), jax.ShapeDtypeStruct((r, 128), jnp.int32)],
        compiler_params=_params(("parallel",)),
        name="out_proj_router",
    )(merged, x, wo, ln2, wr_hi, wr_lo, br)


def _make_gather_kernel(bm, nc):
    def kern(nused_ref, tok_ref, nxt_ref, h_ref, o_ref, buf, sem):
        b = pl.program_id(0)
        slot = b % 2

        def fetch(idx_ref, s):
            def issue(i2, carry):
                for q_ in range(2):
                    i = 2 * i2 + q_
                    src = pl.multiple_of(idx_ref[0, 0, i] * nc, nc)
                    pltpu.make_async_copy(h_ref.at[pl.ds(src, nc)],
                                          buf.at[s, pl.ds(pl.multiple_of(i * nc, nc), nc)],
                                          sem.at[s]).start(priority=q_)
                return carry

            lax.fori_loop(0, bm // 2, issue, 0, unroll=4)

        @pl.when((b == 0) & (nused_ref[0] > 0))
        def _():
            fetch(tok_ref, 0)

        @pl.when(b + 1 < nused_ref[0])
        def _():
            fetch(nxt_ref, 1 - slot)

        @pl.when(b < nused_ref[0])
        def _():
            pltpu.make_async_copy(h_ref.at[pl.ds(0, bm * nc)], buf.at[slot], sem.at[slot]).wait()
            for c_ in range(nc):
                o_ref[:, c_ * 128:(c_ + 1) * 128] = buf[slot, pl.ds(c_, bm, stride=nc), :].astype(BF16)

        @pl.when(b >= nused_ref[0])
        def _():
            o_ref[...] = jnp.zeros(o_ref.shape, BF16)

    return kern


def _moe_gather(h, tok_sorted, nused, bm, nc):
    mp = tok_sorted.shape[0]
    d = nc * 128
    nb = mp // bm
    tok3 = tok_sorted.reshape(nb, 1, bm)
    grid_spec = pltpu.PrefetchScalarGridSpec(
        num_scalar_prefetch=1,
        grid=(nb,),
        in_specs=[pl.BlockSpec((1, 1, bm), lambda b, nu: (b, 0, 0), memory_space=pltpu.SMEM),
                  pl.BlockSpec((1, 1, bm), lambda b, nu: (jnp.minimum(b + 1, nb - 1), 0, 0),
                               memory_space=pltpu.SMEM),
                  pl.BlockSpec(memory_space=pl.ANY)],
        out_specs=pl.BlockSpec((bm, d), lambda b, nu: (b, 0)),
        scratch_shapes=[pltpu.VMEM((2, bm * nc, 128), F32), pltpu.SemaphoreType.DMA((2,))],
    )
    return pl.pallas_call(
        _make_gather_kernel(bm, nc),
        grid_spec=grid_spec,
        out_shape=jax.ShapeDtypeStruct((mp, d), BF16),
        compiler_params=_params(("arbitrary",)),
        name="moe_gather",
    )(nused, tok3, tok3, h)


WK_E, WK_J, WK_B, WK_JO, WK_VALID, WK_FIRST = range(6)


def _moe_up_kernel(wk_ref, x_ref, wg_ref, wu_ref, bg_ref, bu_ref, o_ref, wgb, wub):
    w = pl.program_id(0)

    @pl.when(wk_ref[WK_VALID, w] == 1)
    def _():
        @pl.when(wk_ref[WK_FIRST, w] == 1)
        def _():
            wgb[...] = wg_ref[0].astype(BF16)
            wub[...] = wu_ref[0].astype(BF16)

        x = x_ref[...]
        hg = jnp.dot(x, wgb[...], preferred_element_type=F32) + bg_ref[0]
        hu = jnp.dot(x, wub[...], preferred_element_type=F32) + bu_ref[0]
        gt = jnp.minimum(hg, SWIGLU_LIMIT)
        up = jnp.clip(hu, -SWIGLU_LIMIT, SWIGLU_LIMIT)
        o_ref[...] = ((up + 1.0) * gt * _sigmoid(SWIGLU_ALPHA * gt)).astype(BF16)

    @pl.when(wk_ref[WK_VALID, w] == 0)
    def _():
        o_ref[...] = jnp.zeros(o_ref.shape, BF16)


def _moe_up(xs, w_gu, b_gu, work, bm, tf):
    mp, d = xs.shape
    ne, _, f2 = w_gu.shape
    dff = f2 // 2
    nj = dff // tf
    nwork = work.shape[1]
    grid_spec = pltpu.PrefetchScalarGridSpec(
        num_scalar_prefetch=1,
        grid=(nwork,),
        in_specs=[pl.BlockSpec((bm, d), lambda w, wk: (wk[WK_B, w], 0)),
                  pl.BlockSpec((1, d, tf), lambda w, wk: (wk[WK_E, w], 0, wk[WK_J, w])),
                  pl.BlockSpec((1, d, tf), lambda w, wk: (wk[WK_E, w], 0, nj + wk[WK_J, w])),
                  pl.BlockSpec((1, 1, tf), lambda w, wk: (wk[WK_E, w], 0, wk[WK_J, w])),
                  pl.BlockSpec((1, 1, tf), lambda w, wk: (wk[WK_E, w], 0, nj + wk[WK_J, w]))],
        out_specs=pl.BlockSpec((bm, tf), lambda w, wk: (wk[WK_B, w], wk[WK_JO, w])),
        scratch_shapes=[pltpu.VMEM((d, tf), BF16), pltpu.VMEM((d, tf), BF16)],
    )
    return pl.pallas_call(
        _moe_up_kernel,
        grid_spec=grid_spec,
        out_shape=jax.ShapeDtypeStruct((mp, dff), BF16),
        compiler_params=_params(("arbitrary",)),
        name="moe_up",
    )(work, xs, w_gu, w_gu, b_gu, b_gu)


def _moe_down_kernel(wk_ref, a_ref, wd_ref, bd_ref, o_ref, wdb):
    w = pl.program_id(0)

    @pl.when(wk_ref[WK_VALID, w] == 1)
    def _():
        @pl.when(wk_ref[WK_FIRST, w] == 1)
        def _():
            wdb[...] = wd_ref[0].astype(BF16)

        o_ref[...] = jnp.dot(a_ref[...], wdb[...], preferred_element_type=F32) + bd_ref[0]

    @pl.when(wk_ref[WK_VALID, w] == 0)
    def _():
        o_ref[...] = jnp.zeros(o_ref.shape, F32)


def _moe_down(act, w_down, b_down, work, bm, tn):
    mp, dff = act.shape
    ne, _, d = w_down.shape
    nwork = work.shape[1]
    grid_spec = pltpu.PrefetchScalarGridSpec(
        num_scalar_prefetch=1,
        grid=(nwork,),
        in_specs=[pl.BlockSpec((bm, dff), lambda w, wk: (wk[WK_B, w], 0)),
                  pl.BlockSpec((1, dff, tn), lambda w, wk: (wk[WK_E, w], 0, wk[WK_J, w])),
                  pl.BlockSpec((1, 1, tn), lambda w, wk: (wk[WK_E, w], 0, wk[WK_J, w]))],
        out_specs=pl.BlockSpec((bm, tn), lambda w, wk: (wk[WK_B, w], wk[WK_JO, w])),
        scratch_shapes=[pltpu.VMEM((dff, tn), BF16)],
    )
    return pl.pallas_call(
        _moe_down_kernel,
        grid_spec=grid_spec,
        out_shape=jax.ShapeDtypeStruct((mp, d), F32),
        compiler_params=_params(("arbitrary",)),
        name="moe_down",
    )(work, act, w_down, b_down)


def _make_combine_kernel(tm):
    def kern(pos_ref, nxt_ref, x_ref, g_ref, ys_ref, o_ref, buf, sem):
        i = pl.program_id(0)
        slot = i % 2

        def fetch(idx_ref, s):
            def issue(r, carry):
                for k_ in range(TOP_K):
                    pltpu.make_async_copy(ys_ref.at[pl.ds(idx_ref[0, 0, r * TOP_K + k_], 1)],
                                          buf.at[s, k_, pl.ds(r, 1)], sem.at[s]).start(priority=k_ % 2)
                return carry

            lax.fori_loop(0, tm, issue, 0, unroll=4)

        @pl.when(i == 0)
        def _():
            fetch(pos_ref, 0)

        @pl.when(i + 1 < pl.num_programs(0))
        def _():
            fetch(nxt_ref, 1 - slot)

        for k_ in range(TOP_K):
            pltpu.make_async_copy(ys_ref.at[pl.ds(0, tm)], buf.at[slot, k_], sem.at[slot]).wait()
        g = g_ref[...]
        o_ref[...] = x_ref[...] + ((buf[slot, 0] * g[:, 0:1] + buf[slot, 1] * g[:, 1:2])
                                   + (buf[slot, 2] * g[:, 2:3] + buf[slot, 3] * g[:, 3:4]))

    return kern


def _moe_combine(x1, ys, pos, gate, tm=128):
    r, d = x1.shape
    nt = r // tm
    pos3 = pos.reshape(nt, 1, tm * TOP_K)
    return pl.pallas_call(
        _make_combine_kernel(tm),
        grid=(nt,),
        in_specs=[pl.BlockSpec((1, 1, tm * TOP_K), lambda i: (i, 0, 0), memory_space=pltpu.SMEM),
                  pl.BlockSpec((1, 1, tm * TOP_K), lambda i: (jnp.minimum(i + 1, nt - 1), 0, 0),
                               memory_space=pltpu.SMEM),
                  pl.BlockSpec((tm, d), lambda i: (i, 0)),
                  pl.BlockSpec((tm, 128), lambda i: (i, 0)),
                  pl.BlockSpec(memory_space=pl.ANY)],
        out_specs=pl.BlockSpec((tm, d), lambda i: (i, 0)),
        out_shape=jax.ShapeDtypeStruct((r, d), F32),
        scratch_shapes=[pltpu.VMEM((2, TOP_K, tm, d), F32), pltpu.SemaphoreType.DMA((2,))],
        compiler_params=_params(("arbitrary",)),
        name="moe_combine",
    )(pos3, pos3, x1, gate, ys)


def _moe_routing(idx, segments, n_rows, n_experts, bm, nj):
    n_real = sum(n_ for _, n_ in segments)
    m = n_real * TOP_K
    nb_total = -(-m // bm) + n_experts
    mp = nb_total * bm
    e_flat = jnp.concatenate([idx[s_:s_ + n_] for s_, n_ in segments], axis=0).reshape(m)
    order = jnp.argsort(e_flat)
    rank_all = jnp.argsort(order)
    ex = jnp.arange(n_experts, dtype=jnp.int32)
    onehot = (e_flat[:, None] == ex[None, :]).astype(jnp.int32)
    counts = jnp.sum(onehot, axis=0)
    starts = jnp.cumsum(counts) - counts
    nblk = (counts + bm - 1) // bm
    blk_ends = jnp.cumsum(nblk)
    blk_starts = blk_ends - nblk
    nused = blk_ends[-1]
    base = jnp.sum(onehot * (blk_starts * bm - starts)[None, :], axis=1)
    slot = (base + rank_all).astype(jnp.int32).reshape(n_real, TOP_K)
    pieces, at, off = [], 0, 0
    for s_, n_ in segments:
        pieces += [jnp.zeros((s_ - at, TOP_K), jnp.int32), slot[off:off + n_]]
        at, off = s_ + n_, off + n_
    pos = jnp.concatenate(pieces + [jnp.zeros((n_rows - at, TOP_K), jnp.int32)], axis=0)
    blk_e = jnp.minimum(jnp.sum((blk_ends[None, :] <= jnp.arange(nb_total, dtype=jnp.int32)[:, None])
                                .astype(jnp.int32), axis=1), n_experts - 1)
    blk_first = starts[blk_e] + (jnp.arange(nb_total, dtype=jnp.int32) - blk_starts[blk_e]) * bm
    blk_real = jnp.clip(starts[blk_e] + counts[blk_e] - blk_first, 0, bm)
    lane = jnp.arange(bm, dtype=jnp.int32)[None, :]
    src = order[jnp.clip(blk_first[:, None] + lane, 0, m - 1).reshape(mp)] // TOP_K
    tok, cum, prev_shift = src, 0, 0
    for s_, n_ in segments:
        tok = tok + jnp.where(src >= cum, (s_ - cum) - prev_shift, 0)
        prev_shift, cum = s_ - cum, cum + n_
    tok_sorted = jnp.where(lane < blk_real[:, None], tok.reshape(nb_total, bm), 0).reshape(mp).astype(jnp.int32)
    w = jnp.arange(nb_total * nj, dtype=jnp.int32)
    valid = w < nused * nj
    wc = jnp.minimum(w, jnp.maximum(nused * nj - 1, 0))
    we = blk_e[wc // nj]
    local = wc - nj * blk_starts[we]
    nbe = jnp.maximum(nblk[we], 1)
    wj = local // nbe
    wf = ((local % nbe) == 0) & valid
    wb = jnp.where(valid, blk_starts[we] + local % nbe, w // nj)
    wjo = jnp.where(valid, wj, w % nj)
    work = jnp.stack([we, wj, wb, wjo, valid.astype(jnp.int32), wf.astype(jnp.int32)]).astype(jnp.int32)
    return tok_sorted, pos, nused.reshape(1).astype(jnp.int32), work


def _layout(d, nh_mla, kvl, n_w, n_a, n_g):
    assert d == 2048 and nh_mla * QK_NOPE == d and kvl == 512 and n_g == 256 and n_w <= 128 and n_a <= 128
    cb = {"q_nope": 0, "q_r1": 4, "q_r2": 5, "c": 6, "kpe": 28, "gate_a": 2, "gate_b": 3,
          "r": 4, "k": 5, "v": 6, "lora": 28}
    return {"d": d, "mla_heads": nh_mla, "cb": cb, "nc": 14848}


def _relayout_w_in(w, d, nh, kvl, n_w, n_a, n_g):
    o_w = 3 * d
    o_a = o_w + n_w
    o_g = o_a + n_a
    c1 = o_g + n_g
    c2 = c1 + nh * QK_DIM
    c3 = c2 + kvl
    c4 = c3 + QK_ROPE
    q = w[:, c1:c2].reshape(d, nh, QK_DIM)
    z = lambda n_: jnp.zeros((d, n_), w.dtype)
    half = QK_ROPE // 2
    parts = [q[:, :, :QK_NOPE].reshape(d, nh * QK_NOPE),
             q[:, :, QK_NOPE:QK_NOPE + half].reshape(d, nh * half),
             q[:, :, QK_NOPE + half:].reshape(d, nh * half),
             w[:, c2:c3], w[:, c3:c4], z(128 - QK_ROPE), z(384),
             w[:, c4:c4 + d], w[:, c4 + d:c4 + 2 * d],
             w[:, 0:3 * d],
             w[:, o_w:o_a], z(128 - n_w), w[:, o_a:o_g], z(128 - n_a), w[:, o_g:c1]]
    return jnp.concatenate(parts, axis=1).astype(BF16)


def _pad_lora_vec(vec, d, n_w, n_a, n_g):
    o_w = 3 * d
    o_a = o_w + n_w
    o_g = o_a + n_a
    z = lambda n_: jnp.zeros(vec.shape[:-1] + (n_,), vec.dtype)
    lora = jnp.concatenate([vec[..., o_w:o_a], z(128 - n_w), vec[..., o_a:o_g], z(128 - n_a),
                            vec[..., o_g:o_g + n_g]], axis=-1)
    return vec[..., 0:d], vec[..., d:2 * d], vec[..., 2 * d:3 * d], lora


def _unpad_shift(cols_row, lay, d, n_w, n_a, n_g):
    rkv = cols_row[..., 4 * d:7 * d]
    lo = cols_row[..., 7 * d:7 * d + 512]
    return jnp.concatenate([rkv, lo[..., 0:n_w], lo[..., 128:128 + n_a], lo[..., 256:256 + n_g]], axis=-1)


def _rope_tables(pos, nh):
    half = QK_ROPE // 2
    inv = ROPE_THETA ** (-jnp.arange(half, dtype=F32) / half)
    ang = pos.astype(F32)[:, None] * inv[None, :]
    cos, sin = jnp.cos(ang), jnp.sin(ang)
    z = jnp.zeros((pos.shape[0], 128 - QK_ROPE), F32)
    return {"cos": jnp.tile(cos, (1, nh)), "sin": jnp.tile(sin, (1, nh)),
            "c128": jnp.concatenate([cos, cos, z], axis=1),
            "s128": jnp.concatenate([-sin, sin, z], axis=1)}


def kernel(x_prompt, x_sample, cache_kv_latent, cache_k_rope, state_wkv, state_shift, page_table, meta_tokens,
           ln1_g, w_in, rw_mu, rw_w0, rw_w2, rw_a0, rw_a2, rw_g2, rw_k_k, rw_k_a, rw_r_k, rw_ln_g, rw_ln_b,
           q_norm_g, k_norm_g, kv_norm_g, w_kv_up, w_branch_a, w_branch_b, w_out, ln2_g,
           w_router, b_router, w_gu, b_gu, w_down, b_down):
    depth = w_in.shape[0]
    nb, seq, d = x_prompt.shape
    db, ds, _ = x_sample.shape
    n_meta = meta_tokens.shape[0]
    t = seq + n_meta
    tp = -(-t // ATTN_TILE) * ATTN_TILE
    rp = nb * tp
    rs = db * ds
    r_all = rp + rs
    n_pages = page_table.shape[1]
    page = cache_kv_latent.shape[2]
    kvl = cache_kv_latent.shape[3]
    past = n_pages * page
    nh_rw = d // RWKV_HEAD
    n_w, n_a, n_g = rw_w2.shape[1], rw_a2.shape[1], rw_g2.shape[1]
    nh = w_kv_up.shape[2] // (QK_NOPE + V_HEAD)
    n_experts = w_router.shape[2]
    lay = _layout(d, nh, kvl, n_w, n_a, n_g)
    assert r_all % 1024 == 0 and rs % 256 == 0 and ds <= 8

    meta = meta_tokens.astype(x_prompt.dtype)
    zpad = jnp.zeros((tp - t, d), x_prompt.dtype)
    x = jnp.concatenate([a_ for b_ in range(nb) for a_ in (meta, x_prompt[b_], zpad)] + [x_sample.reshape(rs, d)],
                        axis=0)

    pos_all = jnp.concatenate([jnp.tile(jnp.arange(tp), nb), jnp.tile(past + jnp.arange(ds), db)])
    tabs = _rope_tables(pos_all, nh)
    segments = [(b_ * tp, t) for b_ in range(nb)] + [(rp, rs)]

    outs = {k_: [] for k_ in ("lat_p", "kpe_p", "wkv_p", "sh_p", "lat_s", "kpe_s", "wkv_s", "sh_s")}
    for l in range(depth):
        w_in_l = _relayout_w_in(w_in[l], d, nh, kvl, n_w, n_a, n_g)
        cols = _in_proj(x, ln1_g[l][None], w_in_l)

        mu_r, mu_k, mu_v, mu_l = _pad_lora_vec(rw_mu[l][None], d, n_w, n_a, n_g)
        zrow = lambda a_, n_: jnp.concatenate([a_, jnp.zeros((128 - n_, d), a_.dtype)], axis=0)
        rprm = {"mu_r": mu_r, "mu_k": mu_k, "mu_v": mu_v, "mu_l": mu_l,
                "w0": rw_w0[l][None], "w2": zrow(rw_w2[l], n_w), "a0": rw_a0[l][None], "a2": zrow(rw_a2[l], n_a),
                "g2": rw_g2[l], "k_k": rw_k_k[l][None], "k_a": rw_k_a[l][None],
                "ln_g": rw_ln_g[l][None], "ln_b": rw_ln_b[l][None], "r_k": rw_r_k[l].reshape(1, d)}
        pp = _rwkv_prep(cols, lay, 0, rp, False, None, rprm, tp // 128, t, ds)
        sh = jnp.zeros((db, ds, state_shift.shape[2]), F32).at[:, 0].set(state_shift[l]).reshape(rs, -1)
        sp = _rwkv_prep(cols, lay, rp, rs, True, _pad_lora_vec(sh, d, n_w, n_a, n_g), rprm, 1, ds, ds)
        r_p, lw_p, k_p, v_p, kk_p, b_p, g_p = pp
        r_s, lw_s, k_s, v_s, kk_s, b_s, g_s = sp
        y_p, wkv_p = _wkv_prompt([r_p, lw_p, k_p, v_p, kk_p, b_p], nb, tp, t)
        seq_s = [jnp.pad(a_.reshape(db, ds, d), ((0, 0), (0, 8 - ds), (0, 0))).reshape(db * 8, d)
                 for a_ in (r_s, lw_s, k_s, v_s, kk_s, b_s)]
        y_s, wkv_s = _wkv_sample(seq_s, state_wkv[l], 8)
        y_s = y_s.reshape(db, 8, d)[:, :ds].reshape(rs, d)
        ya = jnp.concatenate([_rwkv_post(y_p, r_p, k_p, v_p, g_p, rprm),
                              _rwkv_post(y_s, r_s, k_s, v_s, g_s, rprm)], axis=0)

        w_up = w_kv_up[l].reshape(kvl, nh, QK_NOPE + V_HEAD)
        gq, gk = q_norm_g[l], k_norm_g[l]
        half = QK_ROPE // 2
        z64 = jnp.zeros((128 - QK_ROPE,), F32)
        mprm = {"gq_n": jnp.tile(gq[:QK_NOPE], nh)[None], "gq_r": jnp.tile(gq[QK_NOPE:], nh)[None],
                "gk_n": jnp.tile(gk[:QK_NOPE], nh)[None], "gk_n1": gk[:QK_NOPE][None],
                "gk_r128": jnp.concatenate([gk[QK_NOPE:], gk[QK_NOPE:], z64])[None],
                "kv_g": kv_norm_g[l][None],
                "w_uk": w_up[:, :, :QK_NOPE].reshape(kvl, nh * QK_NOPE).astype(BF16),
                "w_uv": w_up[:, :, QK_NOPE:].reshape(kvl, nh * V_HEAD).astype(BF16)}
        q_p, c_p, kpe_p, kx_p, vx_p = _mla_prep(cols, lay, 0, rp, True, tabs, mprm)
        q_s, c_s, kpe_s = _mla_prep(cols, lay, rp, rs, False, tabs, mprm)
        o_p = _prompt_attn(q_p, kx_p, vx_p, nb, tp, nh)

        qa, qr = _sample_q(q_s, mprm, nh)
        qa = qa.reshape(db, ds * nh, kvl)
        qr = qr.reshape(db, ds, nh, 128)[..., :QK_ROPE].reshape(db, ds * nh, QK_ROPE)
        pad8 = lambda a_: jnp.concatenate([a_, jnp.zeros((db, 8 - ds, a_.shape[-1]), F32)], axis=1)
        rnew_t = jnp.pad(jnp.swapaxes(kpe_s.reshape(db, ds, QK_ROPE), 1, 2), ((0, 0), (0, 0), (0, 128 - ds)))
        olat = _sample_attn(page_table, mprm["w_uk"].T, qa, qr, pad8(c_s.reshape(db, ds, kvl)), rnew_t,
                            cache_kv_latent[l], jnp.swapaxes(cache_k_rope[l], 1, 2), nh, ds)
        o_s = _sample_o(olat.reshape(rs, nh * kvl), mprm["w_uv"], nh)
        yb = jnp.concatenate([o_p, o_s], axis=0)

        merged = _merge(ya, yb, cols, lay, w_branch_a[l].astype(BF16), w_branch_b[l].astype(BF16))
        wr = jnp.concatenate([w_router[l], jnp.zeros((d, 128 - n_experts), F32)], axis=1)
        wr_hi = wr.astype(BF16)
        wr_lo = (wr - wr_hi.astype(F32)).astype(BF16)
        br = jnp.concatenate([b_router[l], jnp.zeros((128 - n_experts,), F32)])[None]
        x1, hmoe, gate, idx = _out_router(merged, x, w_out[l].astype(BF16), ln2_g[l][None], wr_hi, wr_lo, br, n_experts)

        dff = w_down.shape[2]
        assert d // MOE_TN == dff // MOE_TF
        tok_sorted, pos, nused, work = _moe_routing(idx[:, :TOP_K], segments, r_all, n_experts, MOE_BM,
                                                    dff // MOE_TF)
        xs = _moe_gather(hmoe, tok_sorted, nused, MOE_BM, d // 128)
        act = _moe_up(xs, w_gu[l], b_gu[l][:, None, :], work, MOE_BM, MOE_TF)
        ys = _moe_down(act, w_down[l], b_down[l][:, None, :], work, MOE_BM, MOE_TN)
        x = _moe_combine(x1, ys, pos, gate)

        outs["lat_p"].append(c_p.reshape(nb, tp, kvl)[:, :t])
        outs["kpe_p"].append(kpe_p.reshape(nb, tp, QK_ROPE)[:, :t])
        outs["wkv_p"].append(wkv_p)
        last_p = np.arange(nb, dtype=np.int32) * tp + (t - 1)
        last_s = rp + np.arange(db, dtype=np.int32) * ds + (ds - 1)
        outs["sh_p"].append(_unpad_shift(cols[last_p], lay, d, n_w, n_a, n_g))
        outs["lat_s"].append(c_s.reshape(db, ds, kvl))
        outs["kpe_s"].append(kpe_s.reshape(db, ds, QK_ROPE))
        outs["wkv_s"].append(wkv_s)
        outs["sh_s"].append(_unpad_shift(cols[last_s], lay, d, n_w, n_a, n_g))

    y_prompt = jnp.stack([x[b_ * tp + n_meta:b_ * tp + t] for b_ in range(nb)])
    y_sample = x[rp:].reshape(db, ds, d)
    st = lambda k_: jnp.stack(outs[k_])
    return (y_prompt, y_sample, st("lat_p"), st("kpe_p"), st("wkv_p"), st("sh_p"),
            st("lat_s"), st("kpe_s"), st("wkv_s"), st("sh_s"))
```
